```python
import math
import numpy as np
import jax
import jax.numpy as jnp
from jax import lax

D_MODEL = 1024
BATCH = 16
SEQ = 2048
DEPTH = 4
DEC_BATCH = 128
DEC_SEQ = 4
PAST_LEN = 8192
PAGE_SIZE = 128

HEAD_DIM = 64
MLA_HEADS = 8
Q_RANK = 256
KV_RANK = 128
ROPE_DIM = 32
NOPE_DIM = 64
MLA_V_DIM = 64
ROPE_BASE = 10000.0
FOX_HEADS = 4
FOX_KV_HEADS = 2
FOX_GROUP = FOX_HEADS // FOX_KV_HEADS
FORGET_BIAS_INIT = 4.0
NSA_HEADS = 4
CMP_BLOCK = 32
SEL_BLOCK = 64
SEL_TOPK = 16
WINDOW = 512
FORCE_SCORE = 1000.0
N_BUCKETS = 32
MAX_DISTANCE = 1024
N_KEYS = 128
N_EXPERTS = N_KEYS * N_KEYS
PEER_HEADS = 8
PEER_TOPK = 16
PEER_KEY_DIM = 128
PEER_TOKEN_BLOCK = 256
QBLK = 128
EPS = 1e-6
NEG_INF = -1e30
POS_PAD = 2 ** 30
MLA_SCALE = (NOPE_DIM + ROPE_DIM) ** -0.5
HD_SCALE = HEAD_DIM ** -0.5
MIX_WIDTH = MLA_HEADS * MLA_V_DIM + FOX_HEADS * HEAD_DIM + NSA_HEADS * HEAD_DIM
IN_SPLITS = (Q_RANK, KV_RANK, ROPE_DIM,
             FOX_HEADS * HEAD_DIM, FOX_KV_HEADS * HEAD_DIM, FOX_KV_HEADS * HEAD_DIM, FOX_HEADS,
             NSA_HEADS * HEAD_DIM, HEAD_DIM, HEAD_DIM, HEAD_DIM, HEAD_DIM, HEAD_DIM, HEAD_DIM, 3 * NSA_HEADS)
IN_WIDTH = sum(IN_SPLITS)

kernel_name = 'hybrid_mla_fox_nsa_peer_decode_step'


def rms_norm(x, g):
    xf = x.astype(jnp.float32)
    y = xf * lax.rsqrt(jnp.mean(xf * xf, axis=-1, keepdims=True) + EPS)
    return (y * g.astype(jnp.float32)).astype(x.dtype)


def split_cols(a, sizes):
    return jnp.split(a, np.cumsum(sizes)[:-1].tolist(), axis=-1)


def apply_rope(x, pos):
    half = ROPE_DIM // 2
    inv = ROPE_BASE ** (-jnp.arange(half, dtype=jnp.float32) / half)
    ang = pos.astype(jnp.float32)[:, None] * inv
    ang = ang.reshape((ang.shape[0],) + (1,) * (x.ndim - 3) + (half,))
    cos, sin = jnp.cos(ang), jnp.sin(ang)
    x1 = x[..., :half].astype(jnp.float32)
    x2 = x[..., half:].astype(jnp.float32)
    return jnp.concatenate([x1 * cos - x2 * sin, x1 * sin + x2 * cos], axis=-1).astype(x.dtype)


def t5_bias(table, dist):
    n = jnp.maximum(dist, 0)
    exact = N_BUCKETS // 2
    nf = jnp.maximum(n, 1).astype(jnp.float32)
    large = exact + (jnp.log(nf / exact) / math.log(MAX_DISTANCE / exact) * (N_BUCKETS - exact)).astype(jnp.int32)
    bucket = jnp.where(n < exact, n, jnp.minimum(large, N_BUCKETS - 1))
    return table[bucket].astype(jnp.float32)


def masked_softmax(s, valid):
    p = jax.nn.softmax(jnp.where(valid, s, NEG_INF), axis=-1)
    return jnp.where(valid, p, 0.0)


def sweep_queries(fn, n_q):
    if n_q <= QBLK or n_q % QBLK:
        return fn(0, n_q)
    n_blk = n_q // QBLK
    o = lax.map(lambda s: fn(s, QBLK), jnp.arange(n_blk, dtype=jnp.int32) * QBLK)
    o = jnp.moveaxis(o, 0, 1)
    return o.reshape((o.shape[0], n_q) + o.shape[3:])


def token_mixers(h, past, past_len, lw, t5_table):
    B, S, _ = h.shape
    dt = h.dtype
    q_pos = past_len + jnp.arange(S, dtype=jnp.int32)
    proj = h @ lw['w_in']
    (cq, ckv, kr, fq, fk, fv, ff, nq, kc, vc, ks, vs, kw, vw, ng) = split_cols(proj, IN_SPLITS)
    q = jnp.einsum('bsr,rhe->bshe', rms_norm(cq, lw['g_q']), lw['w_uq'])
    q_rope = apply_rope(q[..., NOPE_DIM:], q_pos)
    q_lat = jnp.einsum('bshn,chn->bshc', q[..., :NOPE_DIM], lw['w_uk'])
    mla_new = jnp.concatenate([rms_norm(ckv, lw['g_kv']), apply_rope(kr, q_pos)], axis=-1)
    fq = fq.reshape(B, S, FOX_KV_HEADS, FOX_GROUP, HEAD_DIM)
    fox_kv_new = jnp.stack([fk.reshape(B, S, FOX_KV_HEADS, HEAD_DIM), fv.reshape(B, S, FOX_KV_HEADS, HEAD_DIM)], axis=2)
    logf_new = jax.nn.log_sigmoid(ff.astype(jnp.float32) + lw['b_f'].astype(jnp.float32)).astype(dt)
    nq = nq.reshape(B, S, NSA_HEADS, HEAD_DIM)
    nsa_kv_new = jnp.stack([kc, vc, ks, vs], axis=2)
    win_new = jnp.stack([kw, vw], axis=2)
    gates = jax.nn.sigmoid(ng.astype(jnp.float32)).reshape(B, S, 3, NSA_HEADS)

    if past is None:
        mla_all, fox_kv_all, logf_all, nsa_all, win_all = mla_new, fox_kv_new, logf_new, nsa_kv_new, win_new
        n_win_past = 0
    else:
        mla_all = jnp.concatenate([past['mla'], mla_new], axis=1)
        fox_kv_all = jnp.concatenate([past['fox_kv'], fox_kv_new], axis=1)
        logf_all = jnp.concatenate([past['logf'], logf_new], axis=1)
        nsa_all = jnp.concatenate([past['nsa'], nsa_kv_new], axis=1)
        win_all = jnp.concatenate([past['win'], win_new], axis=1)
        n_win_past = past['win'].shape[1]
    L = mla_all.shape[1]
    k_pos = jnp.arange(L, dtype=jnp.int32)
    mla_c, mla_kr = mla_all[..., :KV_RANK], mla_all[..., KV_RANK:]
    fox_k, fox_v = fox_kv_all[:, :, 0], fox_kv_all[:, :, 1]
    cum = jnp.cumsum(logf_all.astype(jnp.float32), axis=1)
    q_cum = cum[:, L - S:]
    cum_k = cum.reshape(B, L, FOX_KV_HEADS, FOX_GROUP).transpose(0, 2, 3, 1)[..., None, :]
    n_cmp = L // CMP_BLOCK
    cmp = nsa_all[:, :n_cmp * CMP_BLOCK, :2].astype(jnp.float32).reshape(B, n_cmp, CMP_BLOCK, 2, HEAD_DIM).mean(axis=2).astype(dt)
    k_cmp, v_cmp = cmp[:, :, 0], cmp[:, :, 1]
    cmp_end = jnp.arange(n_cmp, dtype=jnp.int32) * CMP_BLOCK + CMP_BLOCK - 1
    n_sel = -(-L // SEL_BLOCK)
    sel_pad = n_sel * SEL_BLOCK - L
    ratio = SEL_BLOCK // CMP_BLOCK
    top = min(SEL_TOPK, n_sel)
    sel = jnp.pad(nsa_all[:, :, 2:], ((0, 0), (0, sel_pad), (0, 0), (0, 0))).reshape(B, n_sel, SEL_BLOCK, 2, HEAD_DIM)
    sel_pos = jnp.concatenate([k_pos, jnp.full((sel_pad,), POS_PAD, jnp.int32)]).reshape(n_sel, SEL_BLOCK)
    win_pos = past_len - n_win_past + jnp.arange(win_all.shape[1], dtype=jnp.int32)
    banded = past is None and S > QBLK and S % QBLK == 0
    if banded:
        win_keys = jnp.pad(win_all, ((0, 0), (WINDOW, 0), (0, 0), (0, 0)))
        win_kpos = jnp.concatenate([jnp.full((WINDOW,), -POS_PAD, jnp.int32), win_pos])
    else:
        win_keys, win_kpos = win_all, win_pos

    def block(start, size):
        qp = lax.dynamic_slice_in_dim(q_pos, start, size, 0)
        sl = lambda a: lax.dynamic_slice_in_dim(a, start, size, 1)
        causal = k_pos[None, :] <= qp[:, None]
        s = (jnp.einsum('bqhc,bkc->bhqk', sl(q_lat), mla_c)
             + jnp.einsum('bqhr,bkr->bhqk', sl(q_rope), mla_kr)).astype(jnp.float32) * MLA_SCALE
        p = masked_softmax(s, causal).astype(dt)
        o_mla = jnp.einsum('bqhc,chv->bqhv', jnp.einsum('bhqk,bkc->bqhc', p, mla_c), lw['w_uv'])
        o_mla = o_mla.reshape(B, size, MLA_HEADS * MLA_V_DIM)
        s = jnp.einsum('bqkgd,bskd->bkgqs', sl(fq), fox_k).astype(jnp.float32) * HD_SCALE
        qc = sl(q_cum).reshape(B, size, FOX_KV_HEADS, FOX_GROUP).transpose(0, 2, 3, 1)[..., None]
        p = masked_softmax(s + (qc - cum_k), causal).astype(dt)
        o_fox = jnp.einsum('bkgqs,bskd->bqkgd', p, fox_v).reshape(B, size, FOX_HEADS * HEAD_DIM)
        qn = sl(nq)
        dist = qp[:, None] - cmp_end[None, :]
        s = jnp.einsum('bqhd,bnd->bhqn', qn, k_cmp).astype(jnp.float32) * HD_SCALE + t5_bias(t5_table, dist).transpose(2, 0, 1)
        pc = masked_softmax(s, dist >= 0)
        o_cmp = jnp.einsum('bhqn,bnd->bqhd', pc.astype(dt), v_cmp)
        imp = jnp.pad(pc.sum(axis=1), ((0, 0), (0, 0), (0, ratio * n_sel - n_cmp))).reshape(B, size, n_sel, ratio).sum(-1)
        blk = jnp.arange(n_sel, dtype=jnp.int32)
        forced = (blk[None, :] == (qp // SEL_BLOCK)[:, None]) | (blk[None, :] == 0)
        imp = jnp.where(forced, FORCE_SCORE, imp)
        imp = jnp.where(blk[None, :] * SEL_BLOCK > qp[:, None], -1.0, imp)
        _, idx = lax.top_k(imp, top)
        kv_sel = jax.vmap(lambda blocks, ib: blocks[ib])(sel, idx).reshape(B, size, top * SEL_BLOCK, 2, HEAD_DIM)
        pos_s = sel_pos[idx].reshape(B, size, top * SEL_BLOCK)
        dist = qp[None, :, None] - pos_s
        s = jnp.einsum('bqhd,bqmd->bhqm', qn, kv_sel[..., 0, :]).astype(jnp.float32) * HD_SCALE + t5_bias(t5_table, dist).transpose(0, 3, 1, 2)
        ps = masked_softmax(s, (dist >= 0)[:, None])
        o_sel = jnp.einsum('bhqm,bqmd->bqhd', ps.astype(dt), kv_sel[..., 1, :])
        if banded:
            kvw = lax.dynamic_slice_in_dim(win_keys, start, size + WINDOW, 1)
            pw = lax.dynamic_slice_in_dim(win_kpos, start, size + WINDOW, 0)
        else:
            kvw, pw = win_keys, win_kpos
        dist = qp[:, None] - pw[None, :]
        s = jnp.einsum('bqhd,bkd->bhqk', qn, kvw[:, :, 0]).astype(jnp.float32) * HD_SCALE + t5_bias(t5_table, dist).transpose(2, 0, 1)
        pwin = masked_softmax(s, (dist >= 0) & (dist <= WINDOW))
        o_win = jnp.einsum('bhqk,bkd->bqhd', pwin.astype(dt), kvw[:, :, 1])
        g = sl(gates)
        o_nsa = (g[:, :, 0, :, None] * o_cmp + g[:, :, 1, :, None] * o_sel + g[:, :, 2, :, None] * o_win).astype(dt)
        o_nsa = o_nsa.reshape(B, size, NSA_HEADS * HEAD_DIM)
        return jnp.concatenate([o_mla, o_fox, o_nsa], axis=-1)

    o = sweep_queries(block, S)
    win_state = win_all[:, -min(WINDOW, win_all.shape[1]):]
    return o @ lw['w_o'], mla_new, fox_kv_new, logf_new, nsa_kv_new, win_state


def peer_ffn(h, w_pq, sub_keys, expert_u, expert_v):
    B, S, D = h.shape
    n = B * S
    n_blk = -(-n // PEER_TOKEN_BLOCK)
    t = jnp.pad(h.reshape(n, D), ((0, n_blk * PEER_TOKEN_BLOCK - n), (0, 0))).reshape(n_blk, PEER_TOKEN_BLOCK, D)
    kk = PEER_TOPK * PEER_TOPK

    def one(tb):
        q = (tb @ w_pq).reshape(PEER_TOKEN_BLOCK, PEER_HEADS, 2, PEER_KEY_DIM)
        s = jnp.einsum('thpk,hpnk->thpn', q, sub_keys).astype(jnp.float32)
        s1, i1 = lax.top_k(s[:, :, 0], PEER_TOPK)
        s2, i2 = lax.top_k(s[:, :, 1], PEER_TOPK)
        cand = (s1[..., :, None] + s2[..., None, :]).reshape(PEER_TOKEN_BLOCK, PEER_HEADS, kk)
        cidx = (i1[..., :, None] * N_KEYS + i2[..., None, :]).reshape(PEER_TOKEN_BLOCK, PEER_HEADS, kk)
        top_s, top_j = lax.top_k(cand, PEER_TOPK)
        eidx = jnp.take_along_axis(cidx, top_j, axis=-1)
        gate = jax.nn.softmax(top_s, axis=-1)
        act = jax.nn.gelu(jnp.einsum('td,thkd->thk', tb, expert_u[eidx]).astype(jnp.float32), approximate=False)
        return jnp.einsum('thk,thkd->td', (gate * act).astype(tb.dtype), expert_v[eidx])

    return lax.map(one, t).reshape(n_blk * PEER_TOKEN_BLOCK, D)[:n].reshape(B, S, D)


def decoder_layer(x, c, past, past_len, ada_w, ada_b, g_m, g_f, lw, t5_table, fw):
    mod = (jax.nn.silu(c) @ ada_w + ada_b)[:, None, :]
    sh1, sc1, gt1, sh2, sc2, gt2 = jnp.split(mod, 6, axis=-1)
    h = rms_norm(x, g_m) * (1 + sc1) + sh1
    o, mla_r, fkv_r, lf_r, nsa_r, win_s = token_mixers(h, past, past_len, lw, t5_table)
    x = x + gt1 * o
    h = rms_norm(x, g_f) * (1 + sc2) + sh2
    x = x + gt2 * peer_ffn(h, *fw)
    return x, (mla_r, fkv_r, lf_r, nsa_r, win_s)


def setup_inputs(seed: int = 0) -> dict:
    key = jax.random.key(seed)
    ks = jax.random.split(key, 32)
    n_pages = PAST_LEN // PAGE_SIZE
    n_used = DEC_BATCH * n_pages
    n_pool = n_used + n_used // 4
    w_buf = min(WINDOW, PAST_LEN)
    nrm = lambda k, shape, scale=1.0: scale * jax.random.normal(k, shape, jnp.float32)
    page_table = jax.random.permutation(ks[0], n_pool)[:n_used].reshape(DEC_BATCH, n_pages).astype(jnp.int32)
    return {
        'x_prompt': nrm(ks[1], (BATCH, SEQ, D_MODEL)),
        'x_sample': nrm(ks[2], (DEC_BATCH, DEC_SEQ, D_MODEL)),
        'c_prompt': nrm(ks[3], (BATCH, D_MODEL)),
        'c_sample': nrm(ks[4], (DEC_BATCH, D_MODEL)),
        'cache_mla': nrm(ks[5], (DEPTH, n_pool, PAGE_SIZE, KV_RANK + ROPE_DIM)),
        'cache_fox_kv': nrm(ks[6], (DEPTH, n_pool, PAGE_SIZE, 2, FOX_KV_HEADS, HEAD_DIM)),
        'cache_fox_logf': jax.nn.log_sigmoid(FORGET_BIAS_INIT + nrm(ks[7], (DEPTH, n_pool, PAGE_SIZE, FOX_HEADS))),
        'cache_nsa_kv': nrm(ks[8], (DEPTH, n_pool, PAGE_SIZE, 4, HEAD_DIM)),
        'state_nsa_win': nrm(ks[9], (DEPTH, DEC_BATCH, w_buf, 2, HEAD_DIM)),
        'page_table': page_table,
        'w_ada': nrm(ks[10], (DEPTH, D_MODEL, 6 * D_MODEL), 0.5 * D_MODEL ** -0.5),
        'b_ada': nrm(ks[11], (DEPTH, 6 * D_MODEL), 0.01),
        'g_mix': 1.0 + nrm(ks[12], (DEPTH, D_MODEL), 0.1),
        'g_ffn': 1.0 + nrm(ks[13], (DEPTH, D_MODEL), 0.1),
        'g_final': 1.0 + nrm(ks[14], (D_MODEL,), 0.1),
        'w_in': nrm(ks[15], (DEPTH, D_MODEL, IN_WIDTH), D_MODEL ** -0.5),
        'b_f': FORGET_BIAS_INIT + nrm(ks[16], (DEPTH, FOX_HEADS), 0.1),
        'g_q': 1.0 + nrm(ks[17], (DEPTH, Q_RANK), 0.1),
        'g_kv': 1.0 + nrm(ks[18], (DEPTH, KV_RANK), 0.1),
        'w_uq': nrm(ks[19], (DEPTH, Q_RANK, MLA_HEADS, NOPE_DIM + ROPE_DIM), Q_RANK ** -0.5),
        'w_uk': nrm(ks[20], (DEPTH, KV_RANK, MLA_HEADS, NOPE_DIM), KV_RANK ** -0.5),
        'w_uv': nrm(ks[21], (DEPTH, KV_RANK, MLA_HEADS, MLA_V_DIM), KV_RANK ** -0.5),
        'w_o': nrm(ks[22], (DEPTH, MIX_WIDTH, D_MODEL), MIX_WIDTH ** -0.5),
        't5_table': nrm(ks[23], (N_BUCKETS, NSA_HEADS), 0.5),
        'w_pq': nrm(ks[24], (DEPTH, D_MODEL, PEER_HEADS * 2 * PEER_KEY_DIM), D_MODEL ** -0.5),
        'sub_keys': nrm(ks[25], (DEPTH, PEER_HEADS, 2, N_KEYS, PEER_KEY_DIM), PEER_KEY_DIM ** -0.5),
        'expert_u': nrm(ks[26], (DEPTH, N_EXPERTS, D_MODEL), D_MODEL ** -0.5),
        'expert_v': nrm(ks[27], (DEPTH, N_EXPERTS, D_MODEL), PEER_HEADS ** -0.5),
    }


def reference(x_prompt, x_sample, c_prompt, c_sample, cache_mla, cache_fox_kv, cache_fox_logf, cache_nsa_kv,
              state_nsa_win, page_table, w_ada, b_ada, g_mix, g_ffn, g_final, w_in, b_f, g_q, g_kv,
              w_uq, w_uk, w_uv, w_o, t5_table, w_pq, sub_keys, expert_u, expert_v):
    past_len = page_table.shape[1] * PAGE_SIZE

    def gather(cache_l):
        g = cache_l[page_table]
        return g.reshape((g.shape[0], past_len) + g.shape[3:])

    xp, xs = x_prompt, x_sample
    st_p, st_s = [], []
    for l in range(DEPTH):
        lw = {'w_in': w_in[l], 'b_f': b_f[l], 'g_q': g_q[l], 'g_kv': g_kv[l],
              'w_uq': w_uq[l], 'w_uk': w_uk[l], 'w_uv': w_uv[l], 'w_o': w_o[l]}
        fw = (w_pq[l], sub_keys[l], expert_u[l], expert_v[l])
        past = {'mla': gather(cache_mla[l]), 'fox_kv': gather(cache_fox_kv[l]), 'logf': gather(cache_fox_logf[l]),
                'nsa': gather(cache_nsa_kv[l]), 'win': state_nsa_win[l]}
        xp, sp = decoder_layer(xp, c_prompt, None, 0, w_ada[l], b_ada[l], g_mix[l], g_ffn[l], lw, t5_table, fw)
        xs, ss = decoder_layer(xs, c_sample, past, past_len, w_ada[l], b_ada[l], g_mix[l], g_ffn[l], lw, t5_table, fw)
        st_p.append(sp)
        st_s.append(ss)
    return (rms_norm(xp, g_final), rms_norm(xs, g_final),
            jnp.stack([s[0] for s in st_p]), jnp.stack([s[0] for s in st_s]),
            jnp.stack([s[1] for s in st_p]), jnp.stack([s[1] for s in st_s]),
            jnp.stack([s[2] for s in st_p]), jnp.stack([s[2] for s in st_s]),
            jnp.stack([s[3] for s in st_p]), jnp.stack([s[3] for s in st_s]),
            jnp.stack([s[4] for s in st_p]), jnp.stack([s[4] for s in st_s]))
```

```python
import functools
import math

import numpy as np
import jax
import jax.numpy as jnp
from jax import lax
from jax.experimental import pallas as pl
from jax.experimental.pallas import tpu as pltpu

F32 = jnp.float32
BF16 = jnp.bfloat16
I32 = jnp.int32

D_MODEL = 1024
PAGE_SIZE = 128
HEAD_DIM = 64
MLA_HEADS = 8
Q_RANK = 256
KV_RANK = 128
ROPE_DIM = 32
NOPE_DIM = 64
MLA_V_DIM = 64
ROPE_BASE = 10000.0
FOX_HEADS = 4
FOX_KV_HEADS = 2
FOX_GROUP = FOX_HEADS // FOX_KV_HEADS
NSA_HEADS = 4
CMP_BLOCK = 32
SEL_BLOCK = 64
SEL_TOPK = 16
WINDOW = 512
FORCE_SCORE = 1000.0
N_BUCKETS = 32
MAX_DISTANCE = 1024
N_KEYS = 128
N_EXPERTS = N_KEYS * N_KEYS
PEER_HEADS = 8
PEER_TOPK = 16
PEER_KEY_DIM = 128
PEER_PICKS = PEER_HEADS * PEER_TOPK
QBLK = 128
EPS = 1e-6
NEG_INF = -1e30
POS_PAD = 2 ** 30
MLA_SCALE = (NOPE_DIM + ROPE_DIM) ** -0.5
HD_SCALE = HEAD_DIM ** -0.5
MIX_WIDTH = MLA_HEADS * MLA_V_DIM + FOX_HEADS * HEAD_DIM + NSA_HEADS * HEAD_DIM
IN_SPLITS = (Q_RANK, KV_RANK, ROPE_DIM,
             FOX_HEADS * HEAD_DIM, FOX_KV_HEADS * HEAD_DIM, FOX_KV_HEADS * HEAD_DIM, FOX_HEADS,
             NSA_HEADS * HEAD_DIM, HEAD_DIM, HEAD_DIM, HEAD_DIM, HEAD_DIM, HEAD_DIM, HEAD_DIM, 3 * NSA_HEADS)
IN_WIDTH = sum(IN_SPLITS)

SUBLANES = 8
LANES = 128
VMEM_BYTES_V7X = 64 * 1024 * 1024
HALF_EXPERTS = N_EXPERTS // 2


def _cparams(sem, vmem_mb=None):
    kw = dict(dimension_semantics=sem)
    if vmem_mb is not None:
        kw["vmem_limit_bytes"] = vmem_mb * 1024 * 1024
    return pltpu.CompilerParams(**kw)


def _topk_axis0(s, iota, k, payload=None):
    vals, outs = [], []
    for _ in range(k):
        m = jnp.max(s, axis=0, keepdims=True)
        first = jnp.min(jnp.where(s == m, iota, 1e9), axis=0, keepdims=True)
        onehot = iota == first
        if payload is None:
            outs.append(first)
        else:
            outs.append(jnp.max(jnp.where(onehot, payload, -1.0), axis=0, keepdims=True))
        s = jnp.where(onehot, -jnp.inf, s)
        vals.append(m)
    return vals, outs


def _peer_route_kernel(x_ref, sc_ref, sh_ref, g_ref, wpq_ref, keys_ref,
                       h_ref, idx_ref, sft_ref, gate_ref):
    x = x_ref[...]
    y = x * lax.rsqrt(jnp.mean(x * x, axis=-1, keepdims=True) + EPS) * g_ref[...]
    h = y * (1.0 + sc_ref[0]) + sh_ref[0]
    h_ref[...] = h
    q = jnp.dot(h.astype(BF16), wpq_ref[...], preferred_element_type=F32).astype(BF16)
    tm = x.shape[0]
    iota_k = lax.broadcasted_iota(I32, (N_KEYS, tm), 0).astype(F32)
    iota_c = lax.broadcasted_iota(I32, (PEER_TOPK * PEER_TOPK, tm), 0).astype(F32)
    for head in range(PEER_HEADS):
        tops = []
        for p in range(2):
            c = (head * 2 + p) * PEER_KEY_DIM
            s = lax.dot_general(keys_ref[head * 2 + p], q[:, c:c + PEER_KEY_DIM],
                                (((1,), (1,)), ((), ())), preferred_element_type=F32)
            tops.append(_topk_axis0(s, iota_k, PEER_TOPK))
        (v1, i1), (v2, i2) = tops
        v2s = jnp.concatenate(v2, axis=0)
        i2s = jnp.concatenate(i2, axis=0)
        cand = jnp.concatenate([v1[a] + v2s for a in range(PEER_TOPK)], axis=0)
        cidx = jnp.concatenate([i1[a] * float(N_KEYS) + i2s for a in range(PEER_TOPK)], axis=0)
        tv, te = _topk_axis0(cand, iota_c, PEER_TOPK, payload=cidx)
        tv = jnp.concatenate(tv, axis=0)
        te = jnp.concatenate(te, axis=0).astype(I32)
        e = jnp.exp(tv - tv[0:1])
        gate = e / jnp.sum(e, axis=0, keepdims=True)
        rows = slice(head * PEER_TOPK, (head + 1) * PEER_TOPK)
        idx_ref[0, rows, :] = te & (HALF_EXPERTS - 1)
        sft_ref[0, rows, :] = jnp.where(te >= HALF_EXPERTS, 16, 0).astype(I32)
        gate_ref[0, rows, :] = gate


def _peer_route(x, sc, sh, g, wpq, keys, tm):
    n = x.shape[0]
    nblk = n // tm
    bpg = nblk // sc.shape[0]
    mod_spec = pl.BlockSpec((1,) + sc.shape[1:], lambda i: (i // bpg, 0, 0))
    pick_spec = pl.BlockSpec((1, PEER_PICKS, tm), lambda i: (i, 0, 0))
    return pl.pallas_call(
        _peer_route_kernel,
        grid=(nblk,),
        in_specs=[pl.BlockSpec((tm, D_MODEL), lambda i: (i, 0)), mod_spec, mod_spec,
                  pl.BlockSpec((1, D_MODEL), lambda i: (0, 0)),
                  pl.BlockSpec(wpq.shape, lambda i: (0, 0)),
                  pl.BlockSpec(keys.shape, lambda i: (0, 0, 0))],
        out_specs=[pl.BlockSpec((tm, D_MODEL), lambda i: (i, 0)), pick_spec, pick_spec, pick_spec],
        out_shape=[jax.ShapeDtypeStruct((n, D_MODEL), F32),
                   jax.ShapeDtypeStruct((nblk, PEER_PICKS, tm), I32),
                   jax.ShapeDtypeStruct((nblk, PEER_PICKS, tm), I32),
                   jax.ShapeDtypeStruct((nblk, PEER_PICKS, tm), F32)],
        compiler_params=_cparams(("parallel",), 48),
        name="peer_route",
    )(x, sc, sh, g, wpq, keys)


_BITREV3 = (0, 4, 2, 6, 1, 5, 3, 7)


def _expert_row(tbl_ref, r, sh):
    word = tbl_ref[r]
    bits = lax.shift_left(lax.shift_right_logical(word, jnp.full(word.shape, sh, I32)), jnp.full(word.shape, 16, I32))
    return lax.bitcast_convert_type(bits, F32)


def _fold_sublanes(prods, sub):
    lo4 = (sub & 4) == 0
    lo2 = (sub & 2) == 0
    lo1 = (sub & 1) == 0
    q = []
    for a, b in zip(prods[0::2], prods[1::2]):
        q.append(jnp.where(lo4, a, b) + pltpu.roll(jnp.where(lo4, b, a), 4, 0))
    r = []
    for a, b in zip(q[0::2], q[1::2]):
        r.append(jnp.where(lo2, a + pltpu.roll(a, 6, 0), b + pltpu.roll(b, 2, 0)))
    a, b = r
    return jnp.where(lo1, a + pltpu.roll(a, 7, 0), b + pltpu.roll(b, 1, 0))


def _gelu_exact(x):
    return 0.5 * x * (1.0 + lax.erf(x * (2.0 ** -0.5)))


def _peer_act_kernel(idx_ref, sft_ref, h_ref, gate_ref, tbl_ref, w_ref):
    tm = h_ref.shape[0]
    lane = lax.broadcasted_iota(I32, (SUBLANES, LANES), 1)
    sub = lax.broadcasted_iota(I32, (SUBLANES, LANES), 0)

    def token(t, acc):
        h = h_ref[t]
        j = t % SUBLANES
        acc = jnp.where(j == 0, 0.0, acc)
        li = lane - PEER_PICKS // SUBLANES * j
        for g in range(PEER_PICKS // SUBLANES):
            prods = []
            for i in range(SUBLANES):
                p = g * SUBLANES + _BITREV3[i]
                prods.append(_expert_row(tbl_ref, idx_ref[t, p], sft_ref[t, p]) * h)
            col = jnp.sum(_fold_sublanes(prods, sub), axis=1, keepdims=True)
            acc = jnp.where(li == g, col, acc)
        w_ref[t // SUBLANES] = acc
        return acc

    lax.fori_loop(0, tm, token, jnp.zeros((SUBLANES, LANES), F32))
    w_ref[...] = gate_ref[...] * _gelu_exact(w_ref[...])


def _peer_out_kernel(idx_ref, sft_ref, w_ref, x_ref, gt_ref, tbl_ref, o_ref):
    tm = x_ref.shape[0]
    n_acc = 4

    def token(t, carry):
        row0 = (t // SUBLANES) * SUBLANES
        lane0 = (t % SUBLANES) * (PEER_PICKS // SUBLANES)
        accs = [jnp.zeros((SUBLANES, LANES), F32) for _ in range(n_acc)]
        for p in range(PEER_PICKS):
            w = w_ref[row0 + p % SUBLANES, lane0 + p // SUBLANES]
            accs[p % n_acc] = accs[p % n_acc] + w * _expert_row(tbl_ref, idx_ref[t, p], sft_ref[t, p])
        o_ref[t] = (accs[0] + accs[1]) + (accs[2] + accs[3])
        return carry

    lax.fori_loop(0, tm, token, 0)
    o_ref[...] = x_ref[...] + gt_ref[0] * o_ref[...]


def _smem_spec(tm):
    return pl.BlockSpec((tm, PEER_PICKS), lambda i: (i, 0), memory_space=pltpu.SMEM)


def _table_spec():
    return pl.BlockSpec((HALF_EXPERTS, SUBLANES, LANES), lambda i: (0, 0, 0), pipeline_mode=pl.Buffered(1))


def _peer_act(idx, sft, h3, gate3, tbl, tm):
    n = h3.shape[0]
    tile = pl.BlockSpec((tm // SUBLANES, SUBLANES, LANES), lambda i: (i, 0, 0))
    return pl.pallas_call(
        _peer_act_kernel,
        grid=(n // tm,),
        in_specs=[_smem_spec(tm), _smem_spec(tm),
                  pl.BlockSpec((tm, SUBLANES, LANES), lambda i: (i, 0, 0)), tile, _table_spec()],
        out_specs=tile,
        out_shape=jax.ShapeDtypeStruct((n // SUBLANES, SUBLANES, LANES), F32),
        compiler_params=_cparams(("parallel",), 48),
        name="peer_act",
    )(idx, sft, h3, gate3, tbl)


def _peer_out(idx, sft, w2, x3, gt3, tbl, tm):
    n = x3.shape[0]
    bpg = (n // tm) // gt3.shape[0]
    tok = pl.BlockSpec((tm, SUBLANES, LANES), lambda i: (i, 0, 0))
    gt_spec = pl.BlockSpec((1,) + gt3.shape[1:], lambda i: (i // bpg, 0, 0, 0))
    return pl.pallas_call(
        _peer_out_kernel,
        grid=(n // tm,),
        in_specs=[_smem_spec(tm), _smem_spec(tm), _smem_spec(tm), tok, gt_spec, _table_spec()],
        out_specs=tok,
        out_shape=jax.ShapeDtypeStruct((n, SUBLANES, LANES), F32),
        compiler_params=_cparams(("parallel",), 48),
        name="peer_out",
    )(idx, sft, w2, x3, gt3, tbl)


def _pack_table(t):
    b = lax.bitcast_convert_type(t.astype(BF16), jnp.uint16).astype(jnp.uint32)
    packed = b[:HALF_EXPERTS] | (b[HALF_EXPERTS:] << 16)
    return lax.bitcast_convert_type(packed, I32).reshape(HALF_EXPERTS, SUBLANES, LANES)


def _picks_token_major(a):
    nblk, p, tm = a.shape
    return a.transpose(0, 2, 1).reshape(nblk * tm, p)


def _act_layout(a):
    n = a.shape[0]
    g = PEER_PICKS // SUBLANES
    return a.reshape(n // SUBLANES, SUBLANES, g, SUBLANES).transpose(0, 3, 1, 2).reshape(n // SUBLANES, SUBLANES, LANES)


def _peer_block(x, sc2, sh2, gt2, g_f, wpq_bf, keys_bf, tbl_u, tbl_v, tm_route, tm_pass):
    n = x.shape[0]
    h, idx, sft, gate = _peer_route(x, sc2, sh2, g_f, wpq_bf, keys_bf, tm_route)
    idx, sft, gate = _picks_token_major(idx), _picks_token_major(sft), _picks_token_major(gate)
    w = _peer_act(idx, sft, h.reshape(n, SUBLANES, LANES), _act_layout(gate), tbl_u, tm_pass)
    if gt2.shape[1] == 1:
        gt3 = gt2.reshape(gt2.shape[0], 1, SUBLANES, LANES)
    else:
        gt3 = gt2.reshape(n // tm_pass, tm_pass, SUBLANES, LANES)
    out = _peer_out(idx, sft, w.reshape(n, LANES), x.reshape(n, SUBLANES, LANES), gt3, tbl_v, tm_pass)
    return out.reshape(n, D_MODEL)


def _rms(x, g):
    return x * lax.rsqrt(jnp.mean(x * x, axis=-1, keepdims=True) + EPS) * g


def _dot(a, b):
    return jnp.dot(a, b, preferred_element_type=F32)


def _dot_t(a, b):
    return lax.dot_general(a, b, (((1,), (1,)), ((), ())), preferred_element_type=F32)


def _sigmoid(x):
    return 1.0 / (1.0 + jnp.exp(-x))


def _softmax_update(m_ref, l_ref, acc_ref, s, v):
    m_old = m_ref[...]
    m_new = jnp.maximum(m_old, jnp.max(s, axis=1, keepdims=True))
    a = jnp.exp(m_old - m_new)
    p = jnp.exp(s - m_new)
    l_ref[...] = a * l_ref[...] + jnp.sum(p, axis=1, keepdims=True)
    acc_ref[...] = a * acc_ref[...] + _dot(p.astype(BF16), v)
    m_ref[...] = m_new


def _softmax_init(m_ref, l_ref, acc_ref):
    m_ref[...] = jnp.full(m_ref.shape, NEG_INF, F32)
    l_ref[...] = jnp.zeros(l_ref.shape, F32)
    acc_ref[...] = jnp.zeros(acc_ref.shape, F32)


def _topk_mask_lanes(imp, lane, k):
    sel = jnp.zeros(imp.shape, F32)
    for _ in range(k):
        m = jnp.max(imp, axis=1, keepdims=True)
        first = jnp.min(jnp.where(imp == m, lane, 1e9), axis=1, keepdims=True)
        hit = lane == first
        sel = jnp.where(hit, 1.0, sel)
        imp = jnp.where(hit, -jnp.inf, imp)
    return sel


def _ada_kernel(c_ref, w_ref, b_ref, o_ref):
    c = c_ref[...]
    o_ref[...] = jnp.dot(c * _sigmoid(c), w_ref[...], preferred_element_type=F32,
                         precision=lax.Precision.HIGHEST) + b_ref[...]


def _ada(c, w, b):
    nb, d = c.shape
    n_out = w.shape[1]
    tn = 1536
    return pl.pallas_call(
        _ada_kernel,
        grid=(n_out // tn,),
        in_specs=[pl.BlockSpec((nb, d), lambda j: (0, 0)),
                  pl.BlockSpec((d, tn), lambda j: (0, j)),
                  pl.BlockSpec((1, tn), lambda j: (0, j))],
        out_specs=pl.BlockSpec((nb, tn), lambda j: (0, j)),
        out_shape=jax.ShapeDtypeStruct((nb, n_out), F32),
        compiler_params=_cparams(("parallel",), 40),
        name="ada_mod",
    )(c, w, b)


_C_CQ, _C_CKV, _C_GA, _C_GB, _C_FQ, _C_FKV, _C_NQ, _C_NKV, _C_WKV, _C_END = (
    0, 256, 384, 512, 640, 1152, 1408, 1920, 2176, 2304)
_L_LOGF = ROPE_DIM
_L_GATE = ROPE_DIM + FOX_HEADS
_L_GEND = _L_GATE + 3 * NSA_HEADS
_Q_NOPE, _Q_ROPE, _Q_RSW, _Q_END = 0, 512, 1536, 2560


def _proj_kernel(x_ref, sc_ref, sh_ref, gm_ref, w1_ref, gq_ref, gkv_ref, wq_ref, wuk_ref, bf_ref, cm_ref, sm_ref,
                 mla_ref, fkv_ref, nkv_ref, win_ref, misc_ref,
                 qm_ref, kvm_ref, fq_ref, kvf_ref, nq_ref, kvn_ref, kvw_ref, cmp_ref):
    tm = x_ref.shape[0]
    h = _rms(x_ref[...], gm_ref[...]) * (1.0 + sc_ref[0]) + sh_ref[0]
    proj = _dot(h.astype(BF16), w1_ref[...])
    cm = cm_ref[...]
    sm = sm_ref[...]
    cqn = _rms(proj[:, _C_CQ:_C_CKV], gq_ref[...])
    q2 = _dot(cqn.astype(BF16), wq_ref[...])
    qlat = _dot(q2[:, _Q_NOPE:_Q_ROPE].astype(BF16), wuk_ref[...])
    for hd in range(MLA_HEADS):
        lo = hd * LANES
        rot = q2[:, _Q_ROPE + lo:_Q_ROPE + lo + LANES] * cm + q2[:, _Q_RSW + lo:_Q_RSW + lo + LANES] * sm
        qm_ref[:, 2 * lo:2 * lo + LANES] = (qlat[:, lo:lo + LANES] * MLA_SCALE).astype(BF16)
        qm_ref[:, 2 * lo + LANES:2 * lo + 2 * LANES] = (rot * MLA_SCALE).astype(BF16)
    ckvn = _rms(proj[:, _C_CKV:_C_GA], gkv_ref[...])
    ga = proj[:, _C_GA:_C_GB]
    krot = ga * cm + proj[:, _C_GB:_C_FQ] * sm
    mla_ref[:, 0:KV_RANK] = ckvn
    mla_ref[:, KV_RANK:KV_RANK + ROPE_DIM] = krot[:, 0:ROPE_DIM]
    kvm_ref[:, 0:LANES] = ckvn.astype(BF16)
    kvm_ref[:, LANES:2 * LANES] = krot.astype(BF16)
    lane = lax.broadcasted_iota(I32, (tm, LANES), 1)
    z = ga + bf_ref[...]
    logsig = jnp.minimum(z, 0.0) - jnp.log(1.0 + jnp.exp(-jnp.abs(z)))
    misc_ref[...] = jnp.where(lane < _L_LOGF, krot,
                              jnp.where(lane < _L_GATE, logsig,
                                        jnp.where(lane < _L_GEND, _sigmoid(ga), 0.0)))
    fq_ref[...] = (proj[:, _C_FQ:_C_FKV] * HD_SCALE).astype(BF16)
    nq_ref[...] = (proj[:, _C_NQ:_C_NKV] * HD_SCALE).astype(BF16)
    fkv = proj[:, _C_FKV:_C_NQ]
    nkv = proj[:, _C_NKV:_C_WKV]
    wkv = proj[:, _C_WKV:_C_END]
    fkv_ref[...] = fkv
    nkv_ref[...] = nkv
    win_ref[...] = wkv
    kvf_ref[...] = fkv.astype(BF16)
    kvn_ref[...] = nkv.astype(BF16)
    kvw_ref[...] = wkv.astype(BF16)
    cmp_ref[...] = jnp.sum(nkv[:, 0:LANES].reshape(tm // CMP_BLOCK, CMP_BLOCK, LANES), axis=1) * (1.0 / CMP_BLOCK)


def _proj(x, sc, sh, lw, cm, sm, tm):
    n = x.shape[0]
    nblk = n // tm
    bpg = nblk // sc.shape[0]
    pblk = cm.shape[0] // tm
    mod_spec = pl.BlockSpec((1,) + sc.shape[1:], lambda i: (i // bpg, 0, 0))
    const = lambda a: pl.BlockSpec(a.shape, lambda i: (0,) * a.ndim)
    tok = lambda w: pl.BlockSpec((tm, w), lambda i: (i, 0))
    rot_spec = pl.BlockSpec((tm, LANES), lambda i: (i % pblk, 0))
    widths = [(KV_RANK + ROPE_DIM, F32), (2 * LANES, F32), (2 * LANES, F32), (LANES, F32), (LANES, F32),
              (MLA_HEADS * 2 * LANES, BF16), (2 * LANES, BF16), (FOX_HEADS * LANES, BF16), (2 * LANES, BF16),
              (NSA_HEADS * LANES, BF16), (2 * LANES, BF16), (LANES, BF16)]
    out_specs = [tok(w) for w, _ in widths] + [pl.BlockSpec((tm // CMP_BLOCK, LANES), lambda i: (i, 0))]
    out_shape = [jax.ShapeDtypeStruct((n, w), dt) for w, dt in widths] + [
        jax.ShapeDtypeStruct((n // CMP_BLOCK, LANES), F32)]
    return pl.pallas_call(
        _proj_kernel,
        grid=(nblk,),
        in_specs=[tok(D_MODEL), mod_spec, mod_spec, const(lw["g_mix"]), const(lw["w1"]), const(lw["g_q"]),
                  const(lw["g_kv"]), const(lw["wq"]), const(lw["wuk"]), const(lw["bf"]), rot_spec, rot_spec],
        out_specs=out_specs,
        out_shape=out_shape,
        compiler_params=_cparams(("parallel",), 48),
        name="in_proj",
    )(x, sc, sh, lw["g_mix"], lw["w1"], lw["g_q"], lw["g_kv"], lw["wq"], lw["wuk"], lw["bf"], cm, sm)


def _split3(x):
    x1 = x.astype(BF16)
    r = x - x1.astype(F32)
    x2 = r.astype(BF16)
    x3 = (r - x2.astype(F32)).astype(BF16)
    return x1, x2, x3


def _cum_kernel(misc_ref, col_ref, row_ref):
    nblk = misc_ref.shape[1] // LANES
    ri = lax.broadcasted_iota(I32, (LANES, LANES), 0)
    ci = lax.broadcasted_iota(I32, (LANES, LANES), 1)
    tri = jnp.where(ci <= ri, 1.0, 0.0).astype(BF16)
    carry = jnp.zeros((1, LANES), F32)
    for blk in range(nblk):
        x1, x2, x3 = _split3(misc_ref[0, blk * LANES:(blk + 1) * LANES, :])
        c = (_dot(tri, x1) + _dot(tri, x2)) + _dot(tri, x3) + carry
        carry = c[LANES - 1:LANES, :]
        col_ref[0, blk * LANES:(blk + 1) * LANES, :] = c
        row_ref[0, blk] = c.T[_L_LOGF:_L_LOGF + SUBLANES, :]


def _cum(misc3):
    b, s, _ = misc3.shape
    return pl.pallas_call(
        _cum_kernel,
        grid=(b,),
        in_specs=[pl.BlockSpec((1, s, LANES), lambda i: (i, 0, 0))],
        out_specs=[pl.BlockSpec((1, s, LANES), lambda i: (i, 0, 0)),
                   pl.BlockSpec((1, s // LANES, SUBLANES, LANES), lambda i: (i, 0, 0, 0))],
        out_shape=[jax.ShapeDtypeStruct((b, s, LANES), F32),
                   jax.ShapeDtypeStruct((b, s // LANES, SUBLANES, LANES), F32)],
        compiler_params=_cparams(("parallel",)),
        name="fox_cumsum",
    )(misc3)


_T5_EXACT = N_BUCKETS // 2
_T5_THRESH = tuple(int(math.ceil(_T5_EXACT * (MAX_DISTANCE / _T5_EXACT) ** (j / (N_BUCKETS - _T5_EXACT)) - 1e-9))
                   for j in range(1, N_BUCKETS - _T5_EXACT))


def _t5_bias(tbl_ref, dist):
    n = jnp.maximum(dist, 0)
    big = jnp.full(n.shape, _T5_EXACT, I32)
    for t in _T5_THRESH:
        big = big + jnp.where(n >= t, 1, 0)
    bucket = jnp.where(n < _T5_EXACT, n, big)
    outs = [jnp.zeros(n.shape, F32) for _ in range(NSA_HEADS)]
    for j in range(N_BUCKETS):
        hit = bucket == j
        for h in range(NSA_HEADS):
            outs[h] = jnp.where(hit, tbl_ref[j, h], outs[h])
    return outs


def _cmp_block_of_lane(lane, half):
    return jnp.where(lane < half, 2 * lane, 2 * (lane - half) + 1)


def _t5_prompt_kernel(tbl_ref, toep_ref, cmpb_ref):
    nq = toep_ref.shape[1]
    ncmp = cmpb_ref.shape[3]
    i = lax.broadcasted_iota(I32, (QBLK, QBLK), 0)
    j = lax.broadcasted_iota(I32, (QBLK, QBLK), 1)
    ic = lax.broadcasted_iota(I32, (QBLK, ncmp), 0)
    lc = lax.broadcasted_iota(I32, (QBLK, ncmp), 1)
    cmp_end = _cmp_block_of_lane(lc, ncmp // 2) * CMP_BLOCK + (CMP_BLOCK - 1)
    for off in range(nq):
        for h, v in enumerate(_t5_bias(tbl_ref, off * QBLK + i - j)):
            toep_ref[h, off] = v
        for h, v in enumerate(_t5_bias(tbl_ref, off * QBLK + ic - cmp_end)):
            cmpb_ref[h, off] = v


def _t5_prompt(t5_table, s):
    nq = s // QBLK
    ncmp = s // CMP_BLOCK
    return pl.pallas_call(
        _t5_prompt_kernel,
        in_specs=[pl.BlockSpec(memory_space=pltpu.SMEM)],
        out_shape=[jax.ShapeDtypeStruct((NSA_HEADS, nq, QBLK, QBLK), F32),
                   jax.ShapeDtypeStruct((NSA_HEADS, nq, QBLK, ncmp), F32)],
        compiler_params=pltpu.CompilerParams(vmem_limit_bytes=40 * 1024 * 1024),
        name="t5_prompt_tables",
    )(t5_table)


def _t5_decode_kernel(tbl_ref, key_ref, near_ref, cmpb_ref, *, past_len, n_new):
    lp = key_ref.shape[2]
    ncmp = cmpb_ref.shape[2]
    wl = near_ref.shape[2]
    qi = lambda w: lax.broadcasted_iota(I32, (SUBLANES, w), 0)
    ln = lambda w: lax.broadcasted_iota(I32, (SUBLANES, w), 1)
    for h, v in enumerate(_t5_bias(tbl_ref, past_len + qi(lp) - ln(lp))):
        key_ref[h] = v
    for h, v in enumerate(_t5_bias(tbl_ref, WINDOW + qi(wl) - ln(wl))):
        near_ref[h] = v
    cmp_end = _cmp_block_of_lane(ln(ncmp), ncmp // 2) * CMP_BLOCK + (CMP_BLOCK - 1)
    for h, v in enumerate(_t5_bias(tbl_ref, past_len + qi(ncmp) - cmp_end)):
        cmpb_ref[h] = v


def _t5_decode(t5_table, past_len, n_new, w_buf):
    del w_buf
    return pl.pallas_call(
        functools.partial(_t5_decode_kernel, past_len=past_len, n_new=n_new),
        in_specs=[pl.BlockSpec(memory_space=pltpu.SMEM)],
        out_shape=[jax.ShapeDtypeStruct((NSA_HEADS, SUBLANES, past_len), F32),
                   jax.ShapeDtypeStruct((NSA_HEADS, SUBLANES, WINDOW + LANES), F32),
                   jax.ShapeDtypeStruct((NSA_HEADS, SUBLANES, past_len // CMP_BLOCK), F32)],
        compiler_params=pltpu.CompilerParams(vmem_limit_bytes=40 * 1024 * 1024),
        name="t5_decode_tables",
    )(t5_table)


def _attn_prompt_kernel(qm_ref, fq_ref, nq_ref, cumc_ref, misc_ref, cmpb_ref,
                        kvm_ref, kvf_ref, kvn_ref, kvw_ref, cmp_ref, cumr_ref, toep_ref, wuv_ref,
                        mix_ref,
                        q8_scr, fq_scr, nq_scr, cq_scr, selk_scr, m_scr, l_scr, acc_scr):
    qi = pl.program_id(1)
    s_len = kvm_ref.shape[1]
    n_kb = s_len // QBLK
    n_cmp = s_len // CMP_BLOCK
    n_sel = s_len // SEL_BLOCK
    top = min(SEL_TOPK, n_sel)
    q0 = qi * QBLK

    def rows_pos(nh):
        r = lax.broadcasted_iota(I32, (nh * QBLK, QBLK), 0)
        return q0 + (r & (QBLK - 1))

    def key_pos(nh, kb):
        return kb * QBLK + lax.broadcasted_iota(I32, (nh * QBLK, QBLK), 1)

    for h in range(MLA_HEADS):
        q8_scr[h * QBLK:(h + 1) * QBLK, :] = qm_ref[:, h * 2 * LANES:(h + 1) * 2 * LANES]
    cumc = cumc_ref[0]
    for h in range(FOX_HEADS):
        fq_scr[h * QBLK:(h + 1) * QBLK, :] = fq_ref[:, h * LANES:(h + 1) * LANES]
        cq_scr[h * QBLK:(h + 1) * QBLK, :] = jnp.broadcast_to(cumc[:, _L_LOGF + h:_L_LOGF + h + 1], (QBLK, LANES))
    for h in range(NSA_HEADS):
        nq_scr[h * QBLK:(h + 1) * QBLK, :] = nq_ref[:, h * LANES:(h + 1) * LANES]

    nr = MLA_HEADS * QBLK
    m_r, l_r, a_r = m_scr, l_scr, acc_scr
    _softmax_init(m_r, l_r, a_r)
    qp8 = rows_pos(MLA_HEADS)

    def mla_step(kb, c):
        k = kvm_ref[0, pl.ds(pl.multiple_of(kb * QBLK, QBLK), QBLK), :]
        s = _dot_t(q8_scr[...], k)
        s = jnp.where(key_pos(MLA_HEADS, kb) <= qp8, s, NEG_INF)
        _softmax_update(m_r, l_r, a_r, s, k[:, 0:LANES])
        return c

    lax.fori_loop(0, qi + 1, mla_step, 0)
    o_lat = (a_r[...] / l_r[...]).astype(BF16)
    o_mla = _dot(o_lat[0:QBLK], wuv_ref[0])
    for h in range(1, MLA_HEADS):
        o_mla = o_mla + _dot(o_lat[h * QBLK:(h + 1) * QBLK], wuv_ref[h])
    mix_ref[:, 0:MLA_HEADS * MLA_V_DIM] = o_mla.astype(BF16)

    nr = FOX_HEADS * QBLK
    m_r, l_r, a_r = m_scr.at[0:nr], l_scr.at[0:nr], acc_scr.at[0:nr]
    _softmax_init(m_r, l_r, a_r)
    qp4 = rows_pos(FOX_HEADS)

    def fox_step(kb, c):
        k = kvf_ref[0, pl.ds(pl.multiple_of(kb * QBLK, QBLK), QBLK), :]
        ck = cumr_ref[0, kb]
        ck4 = jnp.concatenate([jnp.broadcast_to(ck[h:h + 1, :], (QBLK, LANES)) for h in range(FOX_HEADS)], axis=0)
        s = _dot_t(fq_scr[...], k[:, 0:LANES]) + (cq_scr[...] - ck4)
        s = jnp.where(key_pos(FOX_HEADS, kb) <= qp4, s, NEG_INF)
        _softmax_update(m_r, l_r, a_r, s, k[:, LANES:2 * LANES])
        return c

    lax.fori_loop(0, qi + 1, fox_step, 0)
    o_fox = a_r[...] / l_r[...]
    base = MLA_HEADS * MLA_V_DIM
    for h in range(FOX_HEADS):
        mix_ref[:, base + h * LANES:base + (h + 1) * LANES] = o_fox[h * QBLK:(h + 1) * QBLK].astype(BF16)

    half = n_cmp // 2
    cmpk = jnp.concatenate([cmp_ref[0, pl.ds(0, half, stride=2), :], cmp_ref[0, pl.ds(1, half, stride=2), :]],
                           axis=0).astype(BF16)
    nqv = nq_scr[...]
    bias_c = jnp.concatenate([cmpb_ref[h, 0] for h in range(NSA_HEADS)], axis=0)
    lane_c = lax.broadcasted_iota(I32, (nr, n_cmp), 1)
    cmp_end = _cmp_block_of_lane(lane_c, half) * CMP_BLOCK + (CMP_BLOCK - 1)
    qpc = q0 + (lax.broadcasted_iota(I32, (nr, n_cmp), 0) & (QBLK - 1))
    valid_c = cmp_end <= qpc
    s = jnp.where(valid_c, _dot_t(nqv, cmpk) + bias_c, NEG_INF)
    p = jnp.exp(s - jnp.max(s, axis=1, keepdims=True))
    pc = jnp.where(valid_c, p / jnp.sum(p, axis=1, keepdims=True), 0.0)
    o_cmp = _dot(pc.astype(BF16), cmpk)

    pcs = pc[0:QBLK]
    for h in range(1, NSA_HEADS):
        pcs = pcs + pc[h * QBLK:(h + 1) * QBLK]
    imp = pcs[:, 0:half] + pcs[:, half:n_cmp]
    blk = lax.broadcasted_iota(I32, (QBLK, n_sel), 1)
    qps = q0 + lax.broadcasted_iota(I32, (QBLK, n_sel), 0)
    forced = (blk == qps // SEL_BLOCK) | (blk == 0)
    imp = jnp.where(forced, FORCE_SCORE, imp)
    imp = jnp.where(blk * SEL_BLOCK > qps, -1.0, imp)
    sel = _topk_mask_lanes(imp, blk.astype(F32), top)
    lane_k = lax.broadcasted_iota(I32, (QBLK, QBLK), 1)
    per_kb = QBLK // SEL_BLOCK
    for kb in range(n_kb):
        mk = jnp.zeros((QBLK, QBLK), F32)
        for u in range(per_kb):
            col = jnp.broadcast_to(sel[:, kb * per_kb + u:kb * per_kb + u + 1], (QBLK, QBLK))
            mk = jnp.where(lane_k // SEL_BLOCK == u, col, mk)
        selk_scr[kb] = mk

    def toep4(off):
        return jnp.concatenate([toep_ref[h, off] for h in range(NSA_HEADS)], axis=0)

    _softmax_init(m_r, l_r, a_r)

    def sel_step(kb, c):
        k = kvn_ref[0, pl.ds(pl.multiple_of(kb * QBLK, QBLK), QBLK), :][:, LANES:2 * LANES]
        s = _dot_t(nq_scr[...], k) + toep4(qi - kb)
        mk = selk_scr[kb]
        mk4 = jnp.concatenate([mk] * NSA_HEADS, axis=0)
        s = jnp.where((key_pos(NSA_HEADS, kb) <= qp4) & (mk4 > 0.5), s, NEG_INF)
        _softmax_update(m_r, l_r, a_r, s, k)
        return c

    lax.fori_loop(0, qi + 1, sel_step, 0)
    o_sel = a_r[...] / l_r[...]

    _softmax_init(m_r, l_r, a_r)

    def win_step(kb, c):
        k = kvw_ref[0, pl.ds(pl.multiple_of(kb * QBLK, QBLK), QBLK), :]
        s = _dot_t(nq_scr[...], k) + toep4(qi - kb)
        dist = qp4 - key_pos(NSA_HEADS, kb)
        s = jnp.where((dist >= 0) & (dist <= WINDOW), s, NEG_INF)
        _softmax_update(m_r, l_r, a_r, s, k)
        return c

    lax.fori_loop(jnp.maximum(qi - WINDOW // QBLK, 0), qi + 1, win_step, 0)
    o_win = a_r[...] / l_r[...]

    misc = misc_ref[...]
    base = MLA_HEADS * MLA_V_DIM + FOX_HEADS * LANES
    for h in range(NSA_HEADS):
        rows = slice(h * QBLK, (h + 1) * QBLK)
        gate = lambda c: misc[:, _L_GATE + c * NSA_HEADS + h:_L_GATE + c * NSA_HEADS + h + 1]
        o = gate(0) * o_cmp[rows] + gate(1) * o_sel[rows] + gate(2) * o_win[rows]
        mix_ref[:, base + h * LANES:base + (h + 1) * LANES] = o.astype(BF16)


_MIX_PAD = MLA_HEADS * MLA_V_DIM + (FOX_HEADS + NSA_HEADS) * LANES


def _attn_prompt(po, cumc, cumr, toep, cmpb, wuv, b, s):
    nq = s // QBLK
    qblk = lambda w: pl.BlockSpec((QBLK, w), lambda bi, qi: (bi * nq + qi, 0))
    per_b = lambda a: pl.BlockSpec((1,) + a.shape[1:], lambda bi, qi: (bi,) + (0,) * (a.ndim - 1))
    const = lambda a: pl.BlockSpec(a.shape, lambda bi, qi: (0,) * a.ndim)
    kvm = po["kvm"].reshape(b, s, 2 * LANES)
    kvf = po["kvf"].reshape(b, s, 2 * LANES)
    kvn = po["kvn"].reshape(b, s, 2 * LANES)
    kvw = po["kvw"].reshape(b, s, LANES)
    cmpm = po["cmp"].reshape(b, s // CMP_BLOCK, LANES)
    rows = MLA_HEADS * QBLK
    return pl.pallas_call(
        _attn_prompt_kernel,
        grid=(b, nq),
        in_specs=[qblk(MLA_HEADS * 2 * LANES), qblk(FOX_HEADS * LANES), qblk(NSA_HEADS * LANES),
                  pl.BlockSpec((1, QBLK, LANES), lambda bi, qi: (bi, qi, 0)), qblk(LANES),
                  pl.BlockSpec((NSA_HEADS, 1, QBLK, s // CMP_BLOCK), lambda bi, qi: (0, qi, 0, 0)),
                  per_b(kvm), per_b(kvf), per_b(kvn), per_b(kvw), per_b(cmpm), per_b(cumr),
                  const(toep), const(wuv)],
        out_specs=qblk(_MIX_PAD),
        out_shape=jax.ShapeDtypeStruct((b * s, _MIX_PAD), BF16),
        scratch_shapes=[pltpu.VMEM((rows, 2 * LANES), BF16),
                        pltpu.VMEM((FOX_HEADS * QBLK, LANES), BF16),
                        pltpu.VMEM((NSA_HEADS * QBLK, LANES), BF16),
                        pltpu.VMEM((FOX_HEADS * QBLK, LANES), F32),
                        pltpu.VMEM((nq, QBLK, QBLK), F32),
                        pltpu.VMEM((rows, 1), F32), pltpu.VMEM((rows, 1), F32),
                        pltpu.VMEM((rows, LANES), F32)],
        compiler_params=_cparams(("parallel", "arbitrary"), 48),
        name="attn_prompt",
    )(po["qm"], po["fq"], po["nq"], cumc, po["misc"], cmpb, kvm, kvf, kvn, kvw, cmpm, cumr, toep, wuv)


def _wo_kernel(mix_ref, x_ref, gt_ref, w_ref, o_ref):
    o_ref[...] = x_ref[...] + gt_ref[0] * _dot(mix_ref[...], w_ref[...])


def _wo(mix, x, gt, w, tm):
    n = x.shape[0]
    nblk = n // tm
    bpg = nblk // gt.shape[0]
    return pl.pallas_call(
        _wo_kernel,
        grid=(nblk,),
        in_specs=[pl.BlockSpec((tm, mix.shape[1]), lambda i: (i, 0)),
                  pl.BlockSpec((tm, D_MODEL), lambda i: (i, 0)),
                  pl.BlockSpec((1,) + gt.shape[1:], lambda i: (i // bpg, 0, 0)),
                  pl.BlockSpec(w.shape, lambda i: (0, 0))],
        out_specs=pl.BlockSpec((tm, D_MODEL), lambda i: (i, 0)),
        out_shape=jax.ShapeDtypeStruct((n, D_MODEL), F32),
        compiler_params=_cparams(("parallel",), 40),
        name="out_proj",
    )(mix, x, gt, w)


def _final_norm_kernel(x_ref, g_ref, o_ref):
    o_ref[...] = _rms(x_ref[...], g_ref[...])


def _final_norm(x, g, tm):
    n = x.shape[0]
    return pl.pallas_call(
        _final_norm_kernel,
        grid=(n // tm,),
        in_specs=[pl.BlockSpec((tm, D_MODEL), lambda i: (i, 0)), pl.BlockSpec((1, D_MODEL), lambda i: (0, 0))],
        out_specs=pl.BlockSpec((tm, D_MODEL), lambda i: (i, 0)),
        out_shape=jax.ShapeDtypeStruct((n, D_MODEL), F32),
        compiler_params=_cparams(("parallel",)),
        name="final_norm",
    )(x, g)


_NEW_PAD = 16


def _rows_from_tokens(tok, n_rows, per):
    r = lax.broadcasted_iota(I32, (n_rows, LANES), 0)
    out = jnp.zeros((n_rows, LANES), F32)
    for i in range(n_rows // per):
        out = jnp.where(r // per == i, jnp.broadcast_to(tok[i:i + 1, :], (n_rows, LANES)), out)
    return out


def _pick_lane(x, lane_of_row):
    lane = lax.broadcasted_iota(I32, x.shape, 1)
    return jnp.sum(jnp.where(lane == lane_of_row, x, 0.0), axis=1, keepdims=True)


def _decode1_kernel(pt_ref, qm_ref, fq_ref, newm_ref, newf_ref, cnew_ref, wuv_ref, *rest, pp, n_new):
    del pt_ref
    pages = rest[:4 * pp]
    omla_ref, ofox_ref, cmpm_ref = rest[4 * pp:4 * pp + 3]
    m1, l1, a1, m2, l2, a2, suf, newc = rest[4 * pp + 3:]
    jj = pl.program_id(1)
    r_m = MLA_HEADS * n_new
    r_f = FOX_HEADS * n_new
    row_f = lax.broadcasted_iota(I32, (r_f, LANES), 0)
    lane_f = lax.broadcasted_iota(I32, (r_f, LANES), 1)
    head_lane = _L_LOGF + (row_f & (FOX_HEADS - 1))

    @pl.when(jj == 0)
    def _():
        _softmax_init(m1, l1, a1)
        _softmax_init(m2, l2, a2)
        suf[...] = jnp.zeros(suf.shape, F32)
        x = cnew_ref[0]
        sub = lax.broadcasted_iota(I32, (SUBLANES, LANES), 0)
        y = x + jnp.where(sub >= 1, pltpu.roll(x, 1, 0), 0.0)
        y = y + jnp.where(sub >= 2, pltpu.roll(y, 2, 0), 0.0)
        newc[0:SUBLANES, :] = y
        col = _pick_lane(_rows_from_tokens(y, r_f, FOX_HEADS), head_lane)
        newc[SUBLANES:SUBLANES + r_f, :] = jnp.broadcast_to(col, (r_f, LANES))

    q = qm_ref[0]
    fq = fq_ref[0]
    lane8 = lax.broadcasted_iota(I32, (SUBLANES, LANES), 1)
    for i in range(pp):
        pm, pf, plf, pn = pages[4 * i:4 * i + 4]
        rowm = pm[0, 0]
        c = rowm[:, 0:KV_RANK].astype(BF16)
        kr = rowm[:, KV_RANK:KV_RANK + ROPE_DIM].astype(BF16)
        s = _dot_t(q[:, 0:KV_RANK], c) + _dot_t(q[:, KV_RANK:KV_RANK + ROPE_DIM], kr)
        _softmax_update(m1, l1, a1, s, c)
        rowf = pf[0, 0]
        lf = plf[0, 0]
        y = lf
        for sft in (1, 2, 4, 8, 16, 32, 64):
            y = y + jnp.where(lane8 + sft < LANES, pltpu.roll(y, LANES - sft, 1), 0.0)
        exc = y - lf
        exc8 = exc + pltpu.roll(exc, FOX_HEADS, 0)
        tot8 = jnp.broadcast_to(y[:, 0:1], (SUBLANES, LANES))
        tot8 = tot8 + pltpu.roll(tot8, FOX_HEADS, 0)
        decay = jnp.concatenate([exc8] * (r_f // SUBLANES), axis=0) + suf[...] + newc[SUBLANES:SUBLANES + r_f, :]
        s = _dot_t(fq, rowf[:, 0:LANES].astype(BF16)) + decay
        _softmax_update(m2, l2, a2, s, rowf[:, LANES:2 * LANES].astype(BF16))
        suf[...] = suf[...] + jnp.concatenate([tot8] * (r_f // SUBLANES), axis=0)
        kc = pn[0, 0]
        cmpm_ref[0, pp - 1 - i] = jnp.sum(kc.reshape(PAGE_SIZE // CMP_BLOCK, CMP_BLOCK, LANES), axis=1) * (1.0 / CMP_BLOCK)

    @pl.when(jj == pl.num_programs(1) - 1)
    def _():
        lane_m = lax.broadcasted_iota(I32, (r_m, _NEW_PAD), 1)
        row_m = lax.broadcasted_iota(I32, (r_m, _NEW_PAD), 0)
        kn = newm_ref[0]
        s = jnp.where((lane_m <= row_m // MLA_HEADS) & (lane_m < n_new), _dot_t(q, kn), NEG_INF)
        _softmax_update(m1, l1, a1, s, kn[:, 0:KV_RANK])
        o_lat = (a1[...] / l1[...]).astype(BF16)
        rr = lax.broadcasted_iota(I32, (r_m, LANES), 0)
        o = jnp.zeros((r_m, MLA_HEADS * MLA_V_DIM), F32)
        for h in range(MLA_HEADS):
            o = o + _dot(jnp.where((rr & (MLA_HEADS - 1)) == h, o_lat, jnp.zeros_like(o_lat)), wuv_ref[h])
        omla_ref[0] = jnp.sum(o.reshape(n_new, MLA_HEADS, MLA_HEADS * MLA_V_DIM), axis=1)

        kf = newf_ref[0]
        yc = newc[0:SUBLANES, :]
        ci = newc[SUBLANES:SUBLANES + r_f, :]
        lane_n = lax.broadcasted_iota(I32, (r_f, _NEW_PAD), 1)
        row_n = lax.broadcasted_iota(I32, (r_f, _NEW_PAD), 0)
        decay = jnp.zeros((r_f, _NEW_PAD), F32)
        for j in range(n_new):
            cj = _pick_lane(jnp.broadcast_to(yc[j:j + 1, :], (r_f, LANES)), head_lane)
            decay = jnp.where(lane_n == j, ci[:, 0:_NEW_PAD] - cj, decay)
        s = _dot_t(fq, kf[:, 0:LANES]) + decay
        s = jnp.where((lane_n <= row_n // FOX_HEADS) & (lane_n < n_new), s, NEG_INF)
        _softmax_update(m2, l2, a2, s, kf[:, LANES:2 * LANES])
        ofox_ref[0] = a2[...] / l2[...]


def _decode1(layer, page_table, qm, fq, newm, newf, cnew, wuv, cache_mla, cache_fox, logf_t, cache_nsa, pp):
    b, n_pages = page_table.shape
    n_new = qm.shape[1] // MLA_HEADS
    steps = n_pages // pp
    per_b = lambda a: pl.BlockSpec((1,) + a.shape[1:], lambda bi, jj, pt: (bi,) + (0,) * (a.ndim - 1))

    def page_spec(width, lane_blk, i, rows=PAGE_SIZE):
        def imap(bi, jj, pt):
            return (layer, pt[bi, n_pages - 1 - (jj * pp + i)], 0, lane_blk)
        return pl.BlockSpec((1, 1, rows, width), imap)

    in_specs = [per_b(qm), per_b(fq), per_b(newm), per_b(newf), per_b(cnew),
                pl.BlockSpec(wuv.shape, lambda bi, jj, pt: (0, 0, 0))]
    args = [qm, fq, newm, newf, cnew, wuv]
    for i in range(pp):
        in_specs += [page_spec(KV_RANK + ROPE_DIM, 0, i), page_spec(2 * LANES, 0, i),
                     page_spec(LANES, 0, i, rows=SUBLANES), page_spec(LANES, 0, i)]
        args += [cache_mla, cache_fox, logf_t, cache_nsa]
    r_m, r_f = MLA_HEADS * n_new, FOX_HEADS * n_new
    grid_spec = pltpu.PrefetchScalarGridSpec(
        num_scalar_prefetch=1, grid=(b, steps), in_specs=in_specs,
        out_specs=[pl.BlockSpec((1, n_new, MLA_HEADS * MLA_V_DIM), lambda bi, jj, pt: (bi, 0, 0)),
                   pl.BlockSpec((1, r_f, LANES), lambda bi, jj, pt: (bi, 0, 0)),
                   pl.BlockSpec((1, pp, PAGE_SIZE // CMP_BLOCK, LANES), lambda bi, jj, pt: (bi, steps - 1 - jj, 0, 0))],
        scratch_shapes=[pltpu.VMEM((r_m, 1), F32), pltpu.VMEM((r_m, 1), F32), pltpu.VMEM((r_m, LANES), F32),
                        pltpu.VMEM((r_f, 1), F32), pltpu.VMEM((r_f, 1), F32), pltpu.VMEM((r_f, LANES), F32),
                        pltpu.VMEM((r_f, LANES), F32), pltpu.VMEM((SUBLANES + r_f, LANES), F32)])
    return pl.pallas_call(
        functools.partial(_decode1_kernel, pp=pp, n_new=n_new),
        grid_spec=grid_spec,
        out_shape=[jax.ShapeDtypeStruct((b, n_new, MLA_HEADS * MLA_V_DIM), F32),
                   jax.ShapeDtypeStruct((b, r_f, LANES), F32),
                   jax.ShapeDtypeStruct((b, n_pages, PAGE_SIZE // CMP_BLOCK, LANES), F32)],
        compiler_params=_cparams(("parallel", "arbitrary"), 40),
        name="decode_mla_fox",
    )(page_table, *args)


def _decode2_kernel(pt_ref, nq_ref, misc_ref, cmp_ref, bkey_ref, bnear_ref, bcmp_ref, win_ref, neww_ref, newn_ref,
                    *rest, pp, n_new):
    del pt_ref
    pages = rest[:pp]
    onsa_ref = rest[pp]
    selk, ocmp, m, l, acc = rest[pp + 1:]
    jj = pl.program_id(1)
    n_pages = selk.shape[0]
    r_n = NSA_HEADS * n_new
    nq = nq_ref[0]
    row = lax.broadcasted_iota(I32, (r_n, LANES), 0)
    lane = lax.broadcasted_iota(I32, (r_n, LANES), 1)

    @pl.when(jj == 0)
    def _():
        cmpk = cmp_ref[0].astype(BF16)
        s = _dot_t(nq, cmpk) + bcmp_ref[...]
        p = jnp.exp(s - jnp.max(s, axis=1, keepdims=True))
        pc = p / jnp.sum(p, axis=1, keepdims=True)
        ocmp[...] = _dot(pc.astype(BF16), cmpk)
        gr = lax.broadcasted_iota(I32, (r_n, r_n), 0) // NSA_HEADS
        gc = lax.broadcasted_iota(I32, (r_n, r_n), 1) // NSA_HEADS
        same_q = jnp.where(gr == gc, 1.0, 0.0).astype(BF16)
        p1, p2, p3 = _split3(pc)
        pcs = (_dot(same_q, p1) + _dot(same_q, p2)) + _dot(same_q, p3)
        half = pcs.shape[1] // 2
        imp = pcs[:, 0:half] + pcs[:, half:2 * half]
        lane_b = lax.broadcasted_iota(I32, imp.shape, 1)
        imp = jnp.where(lane_b == 0, FORCE_SCORE, imp)
        n_sel = half + 1
        sel = _topk_mask_lanes(imp, lane_b.astype(F32), min(SEL_TOPK, n_sel) - 1)
        per_pg = PAGE_SIZE // SEL_BLOCK
        for pg in range(n_pages):
            mk = jnp.zeros((r_n, LANES), F32)
            for u in range(per_pg):
                col = jnp.broadcast_to(sel[:, pg * per_pg + u:pg * per_pg + u + 1], (r_n, LANES))
                mk = jnp.where(lane // SEL_BLOCK == u, col, mk)
            selk[pg] = mk
        _softmax_init(m, l, acc)

    for i in range(pp):
        pg = jj * pp + i
        k = pages[i][0, 0].astype(BF16)
        s = _dot_t(nq, k) + bkey_ref[:, i * LANES:(i + 1) * LANES]
        s = jnp.where(selk[pg] > 0.5, s, NEG_INF)
        _softmax_update(m, l, acc, s, k)

    @pl.when(jj == pl.num_programs(1) - 1)
    def _():
        lane_n = lax.broadcasted_iota(I32, (r_n, _NEW_PAD), 1)
        row_n = lax.broadcasted_iota(I32, (r_n, _NEW_PAD), 0)
        own = (lane_n <= row_n // NSA_HEADS) & (lane_n < n_new)
        bnew = bnear_ref[:, WINDOW:WINDOW + LANES][:, 0:_NEW_PAD]
        kn = newn_ref[0]
        s = jnp.where(own, _dot_t(nq, kn) + bnew, NEG_INF)
        _softmax_update(m, l, acc, s, kn)
        o_sel = acc[...] / l[...]
        _softmax_init(m, l, acc)
        kw = win_ref[0].astype(BF16)
        lane_w = lax.broadcasted_iota(I32, (r_n, WINDOW), 1)
        row_w = lax.broadcasted_iota(I32, (r_n, WINDOW), 0)
        s = _dot_t(nq, kw) + bnear_ref[:, 0:WINDOW]
        s = jnp.where(lane_w >= row_w // NSA_HEADS, s, NEG_INF)
        _softmax_update(m, l, acc, s, kw)
        kwn = neww_ref[0]
        s = jnp.where(own, _dot_t(nq, kwn) + bnew, NEG_INF)
        _softmax_update(m, l, acc, s, kwn)
        o_win = acc[...] / l[...]
        g16 = _rows_from_tokens(misc_ref[0], r_n, NSA_HEADS)
        gate = lambda c: _pick_lane(g16, _L_GATE + c * NSA_HEADS + (row & (NSA_HEADS - 1)))
        onsa_ref[0] = gate(0) * ocmp[...] + gate(1) * o_sel + gate(2) * o_win


def _decode2(layer, page_table, nq, misc, cmp_eo, bkey, bnear, bcmp, win, neww, newn, cache_nsa, pp):
    b, n_pages = page_table.shape
    r_n = nq.shape[1]
    n_new = r_n // NSA_HEADS
    steps = n_pages // pp
    per_b = lambda a: pl.BlockSpec((1,) + a.shape[1:], lambda bi, jj, pt: (bi,) + (0,) * (a.ndim - 1))
    const = lambda a: pl.BlockSpec(a.shape, lambda bi, jj, pt: (0,) * a.ndim)
    in_specs = [per_b(nq), per_b(misc), per_b(cmp_eo),
                pl.BlockSpec((r_n, pp * LANES), lambda bi, jj, pt: (0, jj)), const(bnear), const(bcmp),
                per_b(win), per_b(neww), per_b(newn)]
    args = [nq, misc, cmp_eo, bkey, bnear, bcmp, win, neww, newn]
    for i in range(pp):
        in_specs.append(pl.BlockSpec((1, 1, PAGE_SIZE, LANES),
                                     lambda bi, jj, pt, i=i: (layer, pt[bi, jj * pp + i], 0, 1)))
        args.append(cache_nsa)
    grid_spec = pltpu.PrefetchScalarGridSpec(
        num_scalar_prefetch=1, grid=(b, steps), in_specs=in_specs,
        out_specs=pl.BlockSpec((1, r_n, LANES), lambda bi, jj, pt: (bi, 0, 0)),
        scratch_shapes=[pltpu.VMEM((n_pages, r_n, LANES), F32), pltpu.VMEM((r_n, LANES), F32),
                        pltpu.VMEM((r_n, 1), F32), pltpu.VMEM((r_n, 1), F32), pltpu.VMEM((r_n, LANES), F32)])
    return pl.pallas_call(
        functools.partial(_decode2_kernel, pp=pp, n_new=n_new),
        grid_spec=grid_spec,
        out_shape=jax.ShapeDtypeStruct((b, r_n, LANES), F32),
        compiler_params=_cparams(("parallel", "arbitrary"), 40),
        name="decode_nsa",
    )(page_table, *args)


def _rope_swap(w):
    half = ROPE_DIM // 2
    return jnp.concatenate([w[..., half:], w[..., :half]], axis=-1)


def _pad_lanes(w, width=LANES, at=0):
    out = jnp.zeros(w.shape[:-1] + (width,), w.dtype)
    return out.at[..., at:at + w.shape[-1]].set(w)


def _prep_layer(l, w_in, b_f, g_q, g_kv, g_mix, w_uq, w_uk, w_uv, w_o):
    offs = np.concatenate([[0], np.cumsum(IN_SPLITS)])
    col = lambda i: w_in[l][:, offs[i]:offs[i + 1]]
    cq, ckv, kr, fq, fk, fv, ff, nq, kc, vc, ks, vs, kw, vw, ng = [col(i) for i in range(len(IN_SPLITS))]
    ga = _pad_lanes(jnp.concatenate([kr, ff, ng], axis=1))
    gb = _pad_lanes(_rope_swap(kr))
    fq4 = [_pad_lanes(fq[:, h * HEAD_DIM:(h + 1) * HEAD_DIM], at=(h // FOX_GROUP) * HEAD_DIM) for h in range(FOX_HEADS)]
    nq4 = [_pad_lanes(nq[:, h * HEAD_DIM:(h + 1) * HEAD_DIM]) for h in range(NSA_HEADS)]
    w1 = jnp.concatenate([cq, ckv, ga, gb] + fq4 + [fk, fv] + nq4 + [kc, vc, ks, vs, kw, vw], axis=1).astype(BF16)
    uq = w_uq[l]
    nope = uq[:, :, :NOPE_DIM].reshape(Q_RANK, MLA_HEADS * NOPE_DIM)
    rope = _pad_lanes(uq[:, :, NOPE_DIM:]).reshape(Q_RANK, MLA_HEADS * LANES)
    rsw = _pad_lanes(_rope_swap(uq[:, :, NOPE_DIM:])).reshape(Q_RANK, MLA_HEADS * LANES)
    wq = jnp.concatenate([nope, rope, rsw], axis=1).astype(BF16)
    wuk = jnp.zeros((MLA_HEADS * NOPE_DIM, MLA_HEADS * KV_RANK), F32)
    wuv = jnp.zeros((MLA_HEADS, KV_RANK, MLA_HEADS * MLA_V_DIM), F32)
    for h in range(MLA_HEADS):
        wuk = wuk.at[h * NOPE_DIM:(h + 1) * NOPE_DIM, h * KV_RANK:(h + 1) * KV_RANK].set(w_uk[l][:, h, :].T)
        wuv = wuv.at[h, :, h * MLA_V_DIM:(h + 1) * MLA_V_DIM].set(w_uv[l][:, h, :])
    wo = jnp.zeros((_MIX_PAD, D_MODEL), F32)
    n_mla = MLA_HEADS * MLA_V_DIM
    wo = wo.at[0:n_mla].set(w_o[l][0:n_mla])
    for h in range(FOX_HEADS):
        r0 = n_mla + h * LANES + (h // FOX_GROUP) * HEAD_DIM
        wo = wo.at[r0:r0 + HEAD_DIM].set(w_o[l][n_mla + h * HEAD_DIM:n_mla + (h + 1) * HEAD_DIM])
    for h in range(NSA_HEADS):
        r0 = n_mla + (FOX_HEADS + h) * LANES + HEAD_DIM
        src = n_mla + (FOX_HEADS + h) * HEAD_DIM
        wo = wo.at[r0:r0 + HEAD_DIM].set(w_o[l][src:src + HEAD_DIM])
    return dict(w1=w1, wq=wq, wuk=wuk.astype(BF16), wuv=wuv.astype(BF16), wo=wo.astype(BF16),
                g_mix=g_mix[l][None], g_q=g_q[l][None], g_kv=g_kv[l][None],
                bf=_pad_lanes(b_f[l][None], at=_L_LOGF))


def _rope_tables(pos):
    half = ROPE_DIM // 2
    inv = ROPE_BASE ** (-jnp.arange(half, dtype=F32) / half)
    ang = pos.astype(F32)[:, None] * inv
    cos, sin = jnp.cos(ang), jnp.sin(ang)
    return _pad_lanes(jnp.concatenate([cos, cos], axis=1)), _pad_lanes(jnp.concatenate([-sin, sin], axis=1))


_PROJ_NAMES = ("mla", "fkv", "nkv", "win", "misc", "qm", "kvm", "fq", "kvf", "nq", "kvn", "kvw", "cmp")


def _pad_rows(a, rows):
    return jnp.pad(a, ((0, 0), (0, rows - a.shape[1]), (0, 0)))


def kernel(x_prompt, x_sample, c_prompt, c_sample, cache_mla, cache_fox_kv, cache_fox_logf, cache_nsa_kv, state_nsa_win, page_table, w_ada, b_ada, g_mix, g_ffn, g_final, w_in, b_f, g_q, g_kv, w_uq, w_uk, w_uv, w_o, t5_table, w_pq, sub_keys, expert_u, expert_v):
    bp, sp, d = x_prompt.shape
    bs, ss, _ = x_sample.shape
    depth = w_in.shape[0]
    n_pool = cache_mla.shape[1]
    n_pages = page_table.shape[1]
    past_len = n_pages * PAGE_SIZE
    assert state_nsa_win.shape[2] == WINDOW and sp % QBLK == 0 and sp > QBLK
    np_, ns_ = bp * sp, bs * ss
    tm_p = 256
    tm_s = min(256, ns_)
    pp = 4 if n_pages % 4 == 0 else 1

    xp = x_prompt.reshape(np_, d)
    xs = x_sample.reshape(ns_, d)
    cache_fox = cache_fox_kv.reshape(depth, n_pool, PAGE_SIZE, 2 * FOX_KV_HEADS * HEAD_DIM)
    cache_nsa = cache_nsa_kv.reshape(depth, n_pool, PAGE_SIZE, 4 * HEAD_DIM)
    logf_t = _pad_rows(cache_fox_logf.transpose(0, 1, 3, 2).reshape(depth * n_pool, FOX_HEADS, PAGE_SIZE),
                       SUBLANES).reshape(depth, n_pool, SUBLANES, PAGE_SIZE)
    win_buf = state_nsa_win.reshape(depth, bs, WINDOW, 2 * HEAD_DIM)

    cm_p, sm_p = _rope_tables(jnp.arange(sp, dtype=I32))
    cm_s, sm_s = _rope_tables(past_len + jnp.arange(ss, dtype=I32))
    cm_s, sm_s = jnp.tile(cm_s, (tm_s // ss, 1)), jnp.tile(sm_s, (tm_s // ss, 1))
    toep, cmpb = _t5_prompt(t5_table, sp)
    bkey, bnear, bcmp = _t5_decode(t5_table, past_len, ss, WINDOW)
    rows_ih = lambda t: t[:, :ss].transpose(1, 0, 2).reshape(ss * NSA_HEADS, t.shape[2])
    bkey, bnear, bcmp = rows_ih(bkey), rows_ih(bnear), rows_ih(bcmp)

    c_all = jnp.concatenate([c_prompt, c_sample], axis=0)
    st_p, st_s = [], []
    for l in range(depth):
        lw = _prep_layer(l, w_in, b_f, g_q, g_kv, g_mix, w_uq, w_uk, w_uv, w_o)
        mod = _ada(c_all, w_ada[l], b_ada[l][None])
        mod_p = [m[:, None, :] for m in jnp.split(mod[:bp], 6, axis=-1)]
        mod_s = [jnp.repeat(m, ss, axis=0).reshape(ns_ // tm_s, tm_s, d) for m in jnp.split(mod[bp:], 6, axis=-1)]
        wpq = w_pq[l].astype(BF16)
        keys = sub_keys[l].reshape(PEER_HEADS * 2, N_KEYS, PEER_KEY_DIM).astype(BF16)
        tbl_u, tbl_v = _pack_table(expert_u[l]), _pack_table(expert_v[l])

        po = dict(zip(_PROJ_NAMES, _proj(xp, mod_p[1], mod_p[0], lw, cm_p, sm_p, tm_p)))
        cumc, cumr = _cum(po["misc"].reshape(bp, sp, LANES))
        mix = _attn_prompt(po, cumc, cumr, toep, cmpb, lw["wuv"], bp, sp)
        xp = _wo(mix, xp, mod_p[2], lw["wo"], tm_p)
        xp = _peer_block(xp, mod_p[4], mod_p[3], mod_p[5], g_ffn[l][None], wpq, keys, tbl_u, tbl_v, tm_p, 128)
        w_keep = min(WINDOW, sp)
        st_p.append((po["mla"].reshape(bp, sp, -1),
                     po["fkv"].reshape(bp, sp, 2, FOX_KV_HEADS, HEAD_DIM),
                     po["misc"][:, _L_LOGF:_L_GATE].reshape(bp, sp, FOX_HEADS),
                     po["nkv"].reshape(bp, sp, 4, HEAD_DIM),
                     po["win"].reshape(bp, sp, 2, HEAD_DIM)[:, sp - w_keep:]))

        so = dict(zip(_PROJ_NAMES, _proj(xs, mod_s[1], mod_s[0], lw, cm_s, sm_s, tm_s)))
        per_tok = lambda a: a.reshape(bs, ss, a.shape[1])
        newk = lambda a: _pad_rows(per_tok(a), _NEW_PAD)
        qm = so["qm"].reshape(bs, ss * MLA_HEADS, 2 * LANES)
        fq = so["fq"].reshape(bs, ss * FOX_HEADS, LANES)
        nq = so["nq"].reshape(bs, ss * NSA_HEADS, LANES)
        misc8 = _pad_rows(per_tok(so["misc"]), SUBLANES)
        omla, ofox, cmpm = _decode1(l, page_table, qm, fq, newk(so["kvm"]), newk(so["kvf"]), misc8, lw["wuv"],
                                    cache_mla, cache_fox, logf_t, cache_nsa, pp)
        cmpm = cmpm.reshape(bs, n_pages * (PAGE_SIZE // CMP_BLOCK), LANES)
        cmp_eo = jnp.concatenate([cmpm[:, 0::2], cmpm[:, 1::2]], axis=1)
        onsa = _decode2(l, page_table, nq, misc8, cmp_eo, bkey, bnear, bcmp, win_buf[l],
                        newk(so["kvw"]), newk(so["kvn"][:, 2 * HEAD_DIM:]), cache_nsa, pp)
        mix_s = jnp.concatenate([omla.reshape(ns_, -1), ofox.reshape(ns_, -1), onsa.reshape(ns_, -1)],
                                axis=1).astype(BF16)
        xs = _wo(mix_s, xs, mod_s[2], lw["wo"], tm_s)
        xs = _peer_block(xs, mod_s[4], mod_s[3], mod_s[5], g_ffn[l][None], wpq, keys, tbl_u, tbl_v, tm_s, 128)
        win_new = so["win"].reshape(bs, ss, 2, HEAD_DIM)
        win_all = jnp.concatenate([state_nsa_win[l], win_new], axis=1)
        st_s.append((per_tok(so["mla"]),
                     so["fkv"].reshape(bs, ss, 2, FOX_KV_HEADS, HEAD_DIM),
                     so["misc"][:, _L_LOGF:_L_GATE].reshape(bs, ss, FOX_HEADS),
                     so["nkv"].reshape(bs, ss, 4, HEAD_DIM),
                     win_all[:, win_all.shape[1] - min(WINDOW, win_all.shape[1]):]))

    yp = _final_norm(xp, g_final[None], tm_p).reshape(bp, sp, d)
    ys = _final_norm(xs, g_final[None], tm_s).reshape(bs, ss, d)
    stack = lambda st, i: jnp.stack([s[i] for s in st])
    return (yp, ys,
            stack(st_p, 0), stack(st_s, 0), stack(st_p, 1), stack(st_s, 1), stack(st_p, 2), stack(st_s, 2),
            stack(st_p, 3), stack(st_s, 3), stack(st_p, 4), stack(st_s, 4))
```

```python
import functools
import math

import numpy as np
import jax
import jax.numpy as jnp
from jax import lax
from jax.experimental import pallas as pl
from jax.experimental.pallas import tpu as pltpu

F32 = jnp.float32
BF16 = jnp.bfloat16
I32 = jnp.int32

D_MODEL = 1024
PAGE_SIZE = 128
HEAD_DIM = 64
MLA_HEADS = 8
Q_RANK = 256
KV_RANK = 128
ROPE_DIM = 32
NOPE_DIM = 64
MLA_V_DIM = 64
ROPE_BASE = 10000.0
FOX_HEADS = 4
FOX_KV_HEADS = 2
FOX_GROUP = FOX_HEADS // FOX_KV_HEADS
NSA_HEADS = 4
CMP_BLOCK = 32
SEL_BLOCK = 64
SEL_TOPK = 16
WINDOW = 512
FORCE_SCORE = 1000.0
N_BUCKETS = 32
MAX_DISTANCE = 1024
N_KEYS = 128
N_EXPERTS = N_KEYS * N_KEYS
PEER_HEADS = 8
PEER_TOPK = 16
PEER_KEY_DIM = 128
PEER_PICKS = PEER_HEADS * PEER_TOPK
QBLK = 128
EPS = 1e-6
NEG_INF = -1e30
POS_PAD = 2 ** 30
MLA_SCALE = (NOPE_DIM + ROPE_DIM) ** -0.5
HD_SCALE = HEAD_DIM ** -0.5
MIX_WIDTH = MLA_HEADS * MLA_V_DIM + FOX_HEADS * HEAD_DIM + NSA_HEADS * HEAD_DIM
IN_SPLITS = (Q_RANK, KV_RANK, ROPE_DIM,
             FOX_HEADS * HEAD_DIM, FOX_KV_HEADS * HEAD_DIM, FOX_KV_HEADS * HEAD_DIM, FOX_HEADS,
             NSA_HEADS * HEAD_DIM, HEAD_DIM, HEAD_DIM, HEAD_DIM, HEAD_DIM, HEAD_DIM, HEAD_DIM, 3 * NSA_HEADS)
IN_WIDTH = sum(IN_SPLITS)

SUBLANES = 8
LANES = 128
VMEM_BYTES_V7X = 64 * 1024 * 1024
HALF_EXPERTS = N_EXPERTS // 2


def _cparams(sem, vmem_mb=None):
    kw = dict(dimension_semantics=sem)
    if vmem_mb is not None:
        kw["vmem_limit_bytes"] = vmem_mb * 1024 * 1024
    return pltpu.CompilerParams(**kw)


def _topk_axis0(s, iota, k, payload=None):
    vals, outs = [], []
    for _ in range(k):
        m = jnp.max(s, axis=0, keepdims=True)
        first = jnp.min(jnp.where(s == m, iota, 1e9), axis=0, keepdims=True)
        onehot = iota == first
        if payload is None:
            outs.append(first)
        else:
            outs.append(jnp.max(jnp.where(onehot, payload, -1.0), axis=0, keepdims=True))
        s = jnp.where(onehot, -jnp.inf, s)
        vals.append(m)
    return vals, outs


_CAND_ROWS = ((0, PEER_TOPK),) + tuple((a, SUBLANES) for a in range(1, SUBLANES))
_N_CAND = PEER_TOPK + (SUBLANES - 1) * SUBLANES + SUBLANES


def _peer_route_kernel(x_ref, sc_ref, sh_ref, g_ref, wpq_ref, keys_ref,
                       h_ref, idx_ref, hi_ref, gate_ref):
    x = x_ref[...]
    y = x * lax.rsqrt(jnp.mean(x * x, axis=-1, keepdims=True) + EPS) * g_ref[...]
    h = y * (1.0 + sc_ref[0]) + sh_ref[0]
    h_ref[...] = h
    q = jnp.dot(h.astype(BF16), wpq_ref[...], preferred_element_type=F32).astype(BF16)
    tm = x.shape[0]
    iota_k = lax.broadcasted_iota(I32, (N_KEYS, tm), 0).astype(F32)
    rc = lax.broadcasted_iota(I32, (_N_CAND, tm), 0)
    mid = rc - PEER_TOPK
    flat = jnp.where(rc < PEER_TOPK, rc,
                     jnp.where(rc < _N_CAND - SUBLANES,
                               (1 + mid // SUBLANES) * PEER_TOPK + (mid & (SUBLANES - 1)),
                               (rc - (_N_CAND - 2 * SUBLANES)) * PEER_TOPK))
    iota_c = flat.astype(F32)
    for head in range(PEER_HEADS):
        tops = []
        for p in range(2):
            c = (head * 2 + p) * PEER_KEY_DIM
            s = lax.dot_general(keys_ref[head * 2 + p], q[:, c:c + PEER_KEY_DIM],
                                (((1,), (1,)), ((), ())), preferred_element_type=F32)
            tops.append(_topk_axis0(s, iota_k, PEER_TOPK))
        (v1, i1), (v2, i2) = tops
        v1s, i1s = jnp.concatenate(v1, axis=0), jnp.concatenate(i1, axis=0)
        v2s, i2s = jnp.concatenate(v2, axis=0), jnp.concatenate(i2, axis=0)
        cand = jnp.concatenate([v1[a] + v2s[0:nb] for a, nb in _CAND_ROWS] + [v1s[SUBLANES:] + v2[0]], axis=0)
        cidx = jnp.concatenate([i1[a] * float(N_KEYS) + i2s[0:nb] for a, nb in _CAND_ROWS]
                               + [i1s[SUBLANES:] * float(N_KEYS) + i2[0]], axis=0)
        tv, te = _topk_axis0(cand, iota_c, PEER_TOPK, payload=cidx)
        tv = jnp.concatenate(tv, axis=0)
        te = jnp.concatenate(te, axis=0).astype(I32)
        e = jnp.exp(tv - tv[0:1])
        gate = e / jnp.sum(e, axis=0, keepdims=True)
        rows = slice(head * PEER_TOPK, (head + 1) * PEER_TOPK)
        idx_ref[0, rows, :] = (te & (HALF_EXPERTS - 1)) * SUBLANES
        hi_ref[0, rows, :] = jnp.where(te >= HALF_EXPERTS, 1.0, 0.0)
        gate_ref[0, rows, :] = gate


def _peer_route(x, sc, sh, g, wpq, keys, tm):
    n = x.shape[0]
    nblk = n // tm
    bpg = nblk // sc.shape[0]
    mod_spec = pl.BlockSpec((1,) + sc.shape[1:], lambda i: (i // bpg, 0, 0))
    pick_spec = pl.BlockSpec((1, PEER_PICKS, tm), lambda i: (i, 0, 0))
    return pl.pallas_call(
        _peer_route_kernel,
        grid=(nblk,),
        in_specs=[pl.BlockSpec((tm, D_MODEL), lambda i: (i, 0)), mod_spec, mod_spec,
                  pl.BlockSpec((1, D_MODEL), lambda i: (0, 0)),
                  pl.BlockSpec(wpq.shape, lambda i: (0, 0)),
                  pl.BlockSpec(keys.shape, lambda i: (0, 0, 0))],
        out_specs=[pl.BlockSpec((tm, D_MODEL), lambda i: (i, 0)), pick_spec, pick_spec, pick_spec],
        out_shape=[jax.ShapeDtypeStruct((n, D_MODEL), F32),
                   jax.ShapeDtypeStruct((nblk, PEER_PICKS, tm), I32),
                   jax.ShapeDtypeStruct((nblk, PEER_PICKS, tm), F32),
                   jax.ShapeDtypeStruct((nblk, PEER_PICKS, tm), F32)],
        compiler_params=_cparams(("parallel",), 48),
        name="peer_route",
    )(x, sc, sh, g, wpq, keys)


_ROWS_PER_PICK = 2 * SUBLANES
PICK_ROWS = PEER_PICKS * _ROWS_PER_PICK


def _gelu_exact(x):
    return 0.5 * x * (1.0 + lax.erf(x * (2.0 ** -0.5)))


def _pick_expand():
    p = lax.broadcasted_iota(I32, (PEER_PICKS, PICK_ROWS), 0)
    k = lax.broadcasted_iota(I32, (PEER_PICKS, PICK_ROWS), 1)
    return jnp.where(k // _ROWS_PER_PICK == p, 1.0, 0.0).astype(BF16)


def _half_matches(hi_ref, expand):
    tm = hi_ref.shape[0]
    hi_cols = _dot(hi_ref[...].astype(BF16), expand)
    k = lax.broadcasted_iota(I32, (tm, PICK_ROWS), 1)
    return hi_cols == (k & 1).astype(F32)


def _chunk_diag(rows):
    s = lax.broadcasted_iota(I32, (rows, PICK_ROWS), 0)
    k = lax.broadcasted_iota(I32, (rows, PICK_ROWS), 1)
    return s == (k % _ROWS_PER_PICK) // 2


def _gather_rows(idx_ref, tbl_ref, g_scr, t):
    base = t * PEER_PICKS
    for p in range(PEER_PICKS):
        r = pl.multiple_of(idx_ref[base + p], SUBLANES)
        g_scr[p * SUBLANES:(p + 1) * SUBLANES, :] = tbl_ref[pl.ds(r, SUBLANES), :]


def _two_stage_tokens(tm, gather, compute, g0, g1):
    gather(g0, 0)

    def pair(i, c):
        t0 = 2 * i
        gather(g1, t0 + 1)
        compute(g0, t0)
        gather(g0, jnp.minimum(t0 + 2, tm - 1))
        compute(g1, t0 + 1)
        return c

    lax.fori_loop(0, tm // 2, pair, 0)


def _peer_act_kernel(idx_ref, hi_ref, h_ref, gate_ref, tbl_ref, w_ref, g0, g1, rs_scr):
    tm = h_ref.shape[0]
    diag = _chunk_diag(SUBLANES)

    def compute(g_scr, t):
        vt = pltpu.bitcast(g_scr[...], BF16)
        hb = h_ref[t].astype(BF16)
        r = _dot_t(jnp.concatenate([hb, hb], axis=0), vt)[0:SUBLANES]
        rs_scr[pl.ds(t, 1), :] = jnp.sum(jnp.where(diag, r, 0.0), axis=0, keepdims=True)

    _two_stage_tokens(tm, functools.partial(_gather_rows, idx_ref, tbl_ref), compute, g0, g1)
    expand = _pick_expand()
    x1, x2, x3 = _split3(jnp.where(_half_matches(hi_ref, expand), rs_scr[...], 0.0))
    act = (_dot_t(x1, expand) + _dot_t(x2, expand)) + _dot_t(x3, expand)
    w_ref[...] = gate_ref[...] * _gelu_exact(act)


def _peer_out_kernel(idx_ref, hi_ref, w_ref, x_ref, gt_ref, tbl_ref, o_ref, g0, g1, ws_scr):
    tm = x_ref.shape[0]
    expand = _pick_expand()
    w_cols = _dot(w_ref[...].astype(BF16), expand)
    ws_scr[...] = jnp.where(_half_matches(hi_ref, expand), w_cols, 0.0)
    diag = _chunk_diag(_ROWS_PER_PICK)

    def compute(g_scr, t):
        vt = pltpu.bitcast(g_scr[...], BF16)
        wrow = jnp.broadcast_to(ws_scr[pl.ds(t, 1), :], (_ROWS_PER_PICK, PICK_ROWS))
        o_ref[t] = _dot(jnp.where(diag, wrow, 0.0).astype(BF16), vt)[0:SUBLANES]

    _two_stage_tokens(tm, functools.partial(_gather_rows, idx_ref, tbl_ref), compute, g0, g1)
    o_ref[...] = x_ref[...] + gt_ref[0] * o_ref[...]


def _idx_spec(tm):
    return pl.BlockSpec((tm * PEER_PICKS,), lambda i: (i,), memory_space=pltpu.SMEM)


def _table_spec():
    return pl.BlockSpec((HALF_EXPERTS * SUBLANES, LANES), lambda i: (0, 0), pipeline_mode=pl.Buffered(1))


def _gather_scratch():
    return [pltpu.VMEM((PEER_PICKS * SUBLANES, LANES), I32), pltpu.VMEM((PEER_PICKS * SUBLANES, LANES), I32)]


def _peer_act(idx, hi, h3, gate, tbl, tm):
    n = h3.shape[0]
    picks = pl.BlockSpec((tm, PEER_PICKS), lambda i: (i, 0))
    return pl.pallas_call(
        _peer_act_kernel,
        grid=(n // tm,),
        in_specs=[_idx_spec(tm), picks, pl.BlockSpec((tm, SUBLANES, LANES), lambda i: (i, 0, 0)), picks,
                  _table_spec()],
        out_specs=picks,
        out_shape=jax.ShapeDtypeStruct((n, PEER_PICKS), F32),
        scratch_shapes=_gather_scratch() + [pltpu.VMEM((tm, PICK_ROWS), F32)],
        compiler_params=_cparams(("parallel",), 52),
        name="peer_act",
    )(idx, hi, h3, gate, tbl)


def _peer_out(idx, hi, w, x3, gt3, tbl, tm):
    n = x3.shape[0]
    bpg = (n // tm) // gt3.shape[0]
    tok = pl.BlockSpec((tm, SUBLANES, LANES), lambda i: (i, 0, 0))
    picks = pl.BlockSpec((tm, PEER_PICKS), lambda i: (i, 0))
    gt_spec = pl.BlockSpec((1,) + gt3.shape[1:], lambda i: (i // bpg, 0, 0, 0))
    return pl.pallas_call(
        _peer_out_kernel,
        grid=(n // tm,),
        in_specs=[_idx_spec(tm), picks, picks, tok, gt_spec, _table_spec()],
        out_specs=tok,
        out_shape=jax.ShapeDtypeStruct((n, SUBLANES, LANES), F32),
        scratch_shapes=_gather_scratch() + [pltpu.VMEM((tm, PICK_ROWS), F32)],
        compiler_params=_cparams(("parallel",), 52),
        name="peer_out",
    )(idx, hi, w, x3, gt3, tbl)


def _pack_table(t):
    b = lax.bitcast_convert_type(t.astype(BF16), jnp.uint16).astype(jnp.uint32)
    packed = b[:HALF_EXPERTS] | (b[HALF_EXPERTS:] << 16)
    return lax.bitcast_convert_type(packed, I32).reshape(HALF_EXPERTS * SUBLANES, LANES)


def _picks_token_major(a):
    nblk, p, tm = a.shape
    return a.transpose(0, 2, 1).reshape(nblk * tm, p)


def _peer_block(x, sc2, sh2, gt2, g_f, wpq_bf, keys_bf, tbl_u, tbl_v, tm_route, tm_pass):
    n = x.shape[0]
    h, idx, hi, gate = _peer_route(x, sc2, sh2, g_f, wpq_bf, keys_bf, tm_route)
    idx = _picks_token_major(idx).reshape(n * PEER_PICKS)
    hi, gate = _picks_token_major(hi), _picks_token_major(gate)
    w = _peer_act(idx, hi, h.reshape(n, SUBLANES, LANES), gate, tbl_u, tm_pass)
    if gt2.shape[1] == 1:
        gt3 = gt2.reshape(gt2.shape[0], 1, SUBLANES, LANES)
    else:
        gt3 = gt2.reshape(n // tm_pass, tm_pass, SUBLANES, LANES)
    out = _peer_out(idx, hi, w, x.reshape(n, SUBLANES, LANES), gt3, tbl_v, tm_pass)
    return out.reshape(n, D_MODEL)


def _rms(x, g):
    return x * lax.rsqrt(jnp.mean(x * x, axis=-1, keepdims=True) + EPS) * g


def _dot(a, b):
    return jnp.dot(a, b, preferred_element_type=F32)


def _dot_t(a, b):
    return lax.dot_general(a, b, (((1,), (1,)), ((), ())), preferred_element_type=F32)


def _sigmoid(x):
    return 1.0 / (1.0 + jnp.exp(-x))


def _softmax_update(m_ref, l_ref, acc_ref, s, v):
    m_old = m_ref[...]
    m_new = jnp.maximum(m_old, jnp.max(s, axis=1, keepdims=True))
    a = jnp.exp(m_old - m_new)
    p = jnp.exp(s - m_new)
    l_ref[...] = a * l_ref[...] + jnp.sum(p, axis=1, keepdims=True)
    acc_ref[...] = a * acc_ref[...] + _dot(p.astype(BF16), v)
    m_ref[...] = m_new


def _softmax_init(m_ref, l_ref, acc_ref):
    m_ref[...] = jnp.full(m_ref.shape, NEG_INF, F32)
    l_ref[...] = jnp.zeros(l_ref.shape, F32)
    acc_ref[...] = jnp.zeros(acc_ref.shape, F32)


def _topk_mask_lanes(imp, lane, k):
    sel = jnp.zeros(imp.shape, F32)
    for _ in range(k):
        m = jnp.max(imp, axis=1, keepdims=True)
        first = jnp.min(jnp.where(imp == m, lane, 1e9), axis=1, keepdims=True)
        hit = lane == first
        sel = jnp.where(hit, 1.0, sel)
        imp = jnp.where(hit, -jnp.inf, imp)
    return sel


def _ada_kernel(c_ref, w_ref, b_ref, o_ref):
    c = c_ref[...]
    o_ref[...] = jnp.dot(c * _sigmoid(c), w_ref[...], preferred_element_type=F32,
                         precision=lax.Precision.HIGHEST) + b_ref[...]


def _ada(c, w, b):
    nb, d = c.shape
    n_out = w.shape[1]
    tn = 1536
    return pl.pallas_call(
        _ada_kernel,
        grid=(n_out // tn,),
        in_specs=[pl.BlockSpec((nb, d), lambda j: (0, 0)),
                  pl.BlockSpec((d, tn), lambda j: (0, j)),
                  pl.BlockSpec((1, tn), lambda j: (0, j))],
        out_specs=pl.BlockSpec((nb, tn), lambda j: (0, j)),
        out_shape=jax.ShapeDtypeStruct((nb, n_out), F32),
        compiler_params=_cparams(("parallel",), 40),
        name="ada_mod",
    )(c, w, b)


_C_CQ, _C_CKV, _C_GA, _C_GB, _C_FQ, _C_FKV, _C_NQ, _C_NKV, _C_WKV, _C_END = (
    0, 256, 384, 512, 640, 1152, 1408, 1920, 2176, 2304)
_L_LOGF = ROPE_DIM
_L_GATE = ROPE_DIM + FOX_HEADS
_L_GEND = _L_GATE + 3 * NSA_HEADS
_Q_NOPE, _Q_ROPE, _Q_RSW, _Q_END = 0, 512, 1536, 2560


def _proj_kernel(x_ref, sc_ref, sh_ref, gm_ref, w1_ref, gq_ref, gkv_ref, wq_ref, wuk_ref, bf_ref, cm_ref, sm_ref,
                 mla_ref, fkv_ref, nkv_ref, win_ref, misc_ref,
                 qm_ref, kvm_ref, fq_ref, kvf_ref, nq_ref, kvn_ref, kvw_ref, cmp_ref):
    tm = x_ref.shape[0]
    h = _rms(x_ref[...], gm_ref[...]) * (1.0 + sc_ref[0]) + sh_ref[0]
    proj = _dot(h.astype(BF16), w1_ref[...])
    cm = cm_ref[...]
    sm = sm_ref[...]
    cqn = _rms(proj[:, _C_CQ:_C_CKV], gq_ref[...])
    q2 = _dot(cqn.astype(BF16), wq_ref[...])
    qlat = _dot(q2[:, _Q_NOPE:_Q_ROPE].astype(BF16), wuk_ref[...])
    for hd in range(MLA_HEADS):
        lo = hd * LANES
        rot = q2[:, _Q_ROPE + lo:_Q_ROPE + lo + LANES] * cm + q2[:, _Q_RSW + lo:_Q_RSW + lo + LANES] * sm
        qm_ref[:, 2 * lo:2 * lo + LANES] = (qlat[:, lo:lo + LANES] * MLA_SCALE).astype(BF16)
        qm_ref[:, 2 * lo + LANES:2 * lo + 2 * LANES] = (rot * MLA_SCALE).astype(BF16)
    ckvn = _rms(proj[:, _C_CKV:_C_GA], gkv_ref[...])
    ga = proj[:, _C_GA:_C_GB]
    krot = ga * cm + proj[:, _C_GB:_C_FQ] * sm
    mla_ref[:, 0:KV_RANK] = ckvn
    mla_ref[:, KV_RANK:KV_RANK + ROPE_DIM] = krot[:, 0:ROPE_DIM]
    kvm_ref[:, 0:LANES] = ckvn.astype(BF16)
    kvm_ref[:, LANES:2 * LANES] = krot.astype(BF16)
    lane = lax.broadcasted_iota(I32, (tm, LANES), 1)
    z = ga + bf_ref[...]
    logsig = jnp.minimum(z, 0.0) - jnp.log(1.0 + jnp.exp(-jnp.abs(z)))
    misc_ref[...] = jnp.where(lane < _L_LOGF, krot,
                              jnp.where(lane < _L_GATE, logsig,
                                        jnp.where(lane < _L_GEND, _sigmoid(ga), 0.0)))
    fq_ref[...] = (proj[:, _C_FQ:_C_FKV] * HD_SCALE).astype(BF16)
    nq_ref[...] = (proj[:, _C_NQ:_C_NKV] * HD_SCALE).astype(BF16)
    fkv = proj[:, _C_FKV:_C_NQ]
    nkv = proj[:, _C_NKV:_C_WKV]
    wkv = proj[:, _C_WKV:_C_END]
    fkv_ref[...] = fkv
    nkv_ref[...] = nkv
    win_ref[...] = wkv
    kvf_ref[...] = fkv.astype(BF16)
    kvn_ref[...] = nkv.astype(BF16)
    kvw_ref[...] = wkv.astype(BF16)
    cmp_ref[...] = jnp.sum(nkv[:, 0:LANES].reshape(tm // CMP_BLOCK, CMP_BLOCK, LANES), axis=1) * (1.0 / CMP_BLOCK)


def _proj(x, sc, sh, lw, cm, sm, tm):
    n = x.shape[0]
    nblk = n // tm
    bpg = nblk // sc.shape[0]
    pblk = cm.shape[0] // tm
    mod_spec = pl.BlockSpec((1,) + sc.shape[1:], lambda i: (i // bpg, 0, 0))
    const = lambda a: pl.BlockSpec(a.shape, lambda i: (0,) * a.ndim)
    tok = lambda w: pl.BlockSpec((tm, w), lambda i: (i, 0))
    rot_spec = pl.BlockSpec((tm, LANES), lambda i: (i % pblk, 0))
    widths = [(KV_RANK + ROPE_DIM, F32), (2 * LANES, F32), (2 * LANES, F32), (LANES, F32), (LANES, F32),
              (MLA_HEADS * 2 * LANES, BF16), (2 * LANES, BF16), (FOX_HEADS * LANES, BF16), (2 * LANES, BF16),
              (NSA_HEADS * LANES, BF16), (2 * LANES, BF16), (LANES, BF16)]
    out_specs = [tok(w) for w, _ in widths] + [pl.BlockSpec((tm // CMP_BLOCK, LANES), lambda i: (i, 0))]
    out_shape = [jax.ShapeDtypeStruct((n, w), dt) for w, dt in widths] + [
        jax.ShapeDtypeStruct((n // CMP_BLOCK, LANES), F32)]
    return pl.pallas_call(
        _proj_kernel,
        grid=(nblk,),
        in_specs=[tok(D_MODEL), mod_spec, mod_spec, const(lw["g_mix"]), const(lw["w1"]), const(lw["g_q"]),
                  const(lw["g_kv"]), const(lw["wq"]), const(lw["wuk"]), const(lw["bf"]), rot_spec, rot_spec],
        out_specs=out_specs,
        out_shape=out_shape,
        compiler_params=_cparams(("parallel",), 48),
        name="in_proj",
    )(x, sc, sh, lw["g_mix"], lw["w1"], lw["g_q"], lw["g_kv"], lw["wq"], lw["wuk"], lw["bf"], cm, sm)


def _split3(x):
    x1 = x.astype(BF16)
    r = x - x1.astype(F32)
    x2 = r.astype(BF16)
    x3 = (r - x2.astype(F32)).astype(BF16)
    return x1, x2, x3


def _cum_kernel(misc_ref, col_ref, row_ref):
    nblk = misc_ref.shape[1] // LANES
    ri = lax.broadcasted_iota(I32, (LANES, LANES), 0)
    ci = lax.broadcasted_iota(I32, (LANES, LANES), 1)
    tri = jnp.where(ci <= ri, 1.0, 0.0).astype(BF16)
    carry = jnp.zeros((1, LANES), F32)
    for blk in range(nblk):
        x1, x2, x3 = _split3(misc_ref[0, blk * LANES:(blk + 1) * LANES, :])
        c = (_dot(tri, x1) + _dot(tri, x2)) + _dot(tri, x3) + carry
        carry = c[LANES - 1:LANES, :]
        col_ref[0, blk * LANES:(blk + 1) * LANES, :] = c
        row_ref[0, blk] = c.T[_L_LOGF:_L_LOGF + SUBLANES, :]


def _cum(misc3):
    b, s, _ = misc3.shape
    return pl.pallas_call(
        _cum_kernel,
        grid=(b,),
        in_specs=[pl.BlockSpec((1, s, LANES), lambda i: (i, 0, 0))],
        out_specs=[pl.BlockSpec((1, s, LANES), lambda i: (i, 0, 0)),
                   pl.BlockSpec((1, s // LANES, SUBLANES, LANES), lambda i: (i, 0, 0, 0))],
        out_shape=[jax.ShapeDtypeStruct((b, s, LANES), F32),
                   jax.ShapeDtypeStruct((b, s // LANES, SUBLANES, LANES), F32)],
        compiler_params=_cparams(("parallel",)),
        name="fox_cumsum",
    )(misc3)


_T5_EXACT = N_BUCKETS // 2
_T5_THRESH = tuple(int(math.ceil(_T5_EXACT * (MAX_DISTANCE / _T5_EXACT) ** (j / (N_BUCKETS - _T5_EXACT)) - 1e-9))
                   for j in range(1, N_BUCKETS - _T5_EXACT))


def _t5_bias(tbl_ref, dist):
    n = jnp.maximum(dist, 0)
    big = jnp.full(n.shape, _T5_EXACT, I32)
    for t in _T5_THRESH:
        big = big + jnp.where(n >= t, 1, 0)
    bucket = jnp.where(n < _T5_EXACT, n, big)
    outs = [jnp.zeros(n.shape, F32) for _ in range(NSA_HEADS)]
    for j in range(N_BUCKETS):
        hit = bucket == j
        for h in range(NSA_HEADS):
            outs[h] = jnp.where(hit, tbl_ref[j, h], outs[h])
    return outs


def _cmp_block_of_lane(lane, half):
    return jnp.where(lane < half, 2 * lane, 2 * (lane - half) + 1)


def _t5_prompt_kernel(tbl_ref, toep_ref, cmpb_ref):
    nq = toep_ref.shape[1]
    ncmp = cmpb_ref.shape[3]
    i = lax.broadcasted_iota(I32, (QBLK, QBLK), 0)
    j = lax.broadcasted_iota(I32, (QBLK, QBLK), 1)
    ic = lax.broadcasted_iota(I32, (QBLK, ncmp), 0)
    lc = lax.broadcasted_iota(I32, (QBLK, ncmp), 1)
    cmp_end = _cmp_block_of_lane(lc, ncmp // 2) * CMP_BLOCK + (CMP_BLOCK - 1)
    for off in range(nq):
        for h, v in enumerate(_t5_bias(tbl_ref, off * QBLK + i - j)):
            toep_ref[h, off] = v
        for h, v in enumerate(_t5_bias(tbl_ref, off * QBLK + ic - cmp_end)):
            cmpb_ref[h, off] = v


def _t5_prompt(t5_table, s):
    nq = s // QBLK
    ncmp = s // CMP_BLOCK
    return pl.pallas_call(
        _t5_prompt_kernel,
        in_specs=[pl.BlockSpec(memory_space=pltpu.SMEM)],
        out_shape=[jax.ShapeDtypeStruct((NSA_HEADS, nq, QBLK, QBLK), F32),
                   jax.ShapeDtypeStruct((NSA_HEADS, nq, QBLK, ncmp), F32)],
        compiler_params=pltpu.CompilerParams(vmem_limit_bytes=40 * 1024 * 1024),
        name="t5_prompt_tables",
    )(t5_table)


def _t5_decode_kernel(tbl_ref, key_ref, near_ref, cmpb_ref, *, past_len, n_new):
    lp = key_ref.shape[2]
    ncmp = cmpb_ref.shape[2]
    wl = near_ref.shape[2]
    qi = lambda w: lax.broadcasted_iota(I32, (SUBLANES, w), 0)
    ln = lambda w: lax.broadcasted_iota(I32, (SUBLANES, w), 1)
    for h, v in enumerate(_t5_bias(tbl_ref, past_len + qi(lp) - ln(lp))):
        key_ref[h] = v
    for h, v in enumerate(_t5_bias(tbl_ref, WINDOW + qi(wl) - ln(wl))):
        near_ref[h] = v
    cmp_end = _cmp_block_of_lane(ln(ncmp), ncmp // 2) * CMP_BLOCK + (CMP_BLOCK - 1)
    for h, v in enumerate(_t5_bias(tbl_ref, past_len + qi(ncmp) - cmp_end)):
        cmpb_ref[h] = v


def _t5_decode(t5_table, past_len, n_new, w_buf):
    del w_buf
    return pl.pallas_call(
        functools.partial(_t5_decode_kernel, past_len=past_len, n_new=n_new),
        in_specs=[pl.BlockSpec(memory_space=pltpu.SMEM)],
        out_shape=[jax.ShapeDtypeStruct((NSA_HEADS, SUBLANES, past_len), F32),
                   jax.ShapeDtypeStruct((NSA_HEADS, SUBLANES, WINDOW + LANES), F32),
                   jax.ShapeDtypeStruct((NSA_HEADS, SUBLANES, past_len // CMP_BLOCK), F32)],
        compiler_params=pltpu.CompilerParams(vmem_limit_bytes=40 * 1024 * 1024),
        name="t5_decode_tables",
    )(t5_table)


def _attn_prompt_kernel(qm_ref, fq_ref, nq_ref, cumc_ref, misc_ref, cmpb_ref,
                        kvm_ref, kvf_ref, kvn_ref, kvw_ref, cmp_ref, cumr_ref, toep_ref, wuv_ref, eblk_ref,
                        mix_ref,
                        q8_scr, fq_scr, nq_scr, cq_scr, selk_scr, m_scr, l_scr, acc_scr):
    qi = pl.program_id(1)
    s_len = kvm_ref.shape[1]
    n_kb = s_len // QBLK
    n_cmp = s_len // CMP_BLOCK
    n_sel = s_len // SEL_BLOCK
    top = min(SEL_TOPK, n_sel)
    q0 = qi * QBLK

    def rows_pos(nh):
        r = lax.broadcasted_iota(I32, (nh * QBLK, QBLK), 0)
        return q0 + (r & (QBLK - 1))

    def key_pos(nh, kb):
        return kb * QBLK + lax.broadcasted_iota(I32, (nh * QBLK, QBLK), 1)

    for h in range(MLA_HEADS):
        q8_scr[h * QBLK:(h + 1) * QBLK, :] = qm_ref[:, h * 2 * LANES:(h + 1) * 2 * LANES]
    cumc = cumc_ref[0]
    for h in range(FOX_HEADS):
        fq_scr[h * QBLK:(h + 1) * QBLK, :] = fq_ref[:, h * LANES:(h + 1) * LANES]
        cq_scr[h * QBLK:(h + 1) * QBLK, :] = jnp.broadcast_to(cumc[:, _L_LOGF + h:_L_LOGF + h + 1], (QBLK, LANES))
    for h in range(NSA_HEADS):
        nq_scr[h * QBLK:(h + 1) * QBLK, :] = nq_ref[:, h * LANES:(h + 1) * LANES]

    nr = MLA_HEADS * QBLK
    m_r, l_r, a_r = m_scr, l_scr, acc_scr
    _softmax_init(m_r, l_r, a_r)
    qp8 = rows_pos(MLA_HEADS)

    def mla_step(kb, diagonal):
        k = kvm_ref[0, pl.ds(pl.multiple_of(kb * QBLK, QBLK), QBLK), :]
        s = _dot_t(q8_scr[...], k)
        if diagonal:
            s = jnp.where(key_pos(MLA_HEADS, kb) <= qp8, s, NEG_INF)
        _softmax_update(m_r, l_r, a_r, s, k[:, 0:LANES])

    def mla_body(kb, c):
        mla_step(kb, False)
        return c

    lax.fori_loop(0, qi, mla_body, 0)
    mla_step(qi, True)
    o_lat = (a_r[...] / l_r[...]).astype(BF16)
    o_mla = _dot(o_lat[0:QBLK], wuv_ref[0])
    for h in range(1, MLA_HEADS):
        o_mla = o_mla + _dot(o_lat[h * QBLK:(h + 1) * QBLK], wuv_ref[h])
    mix_ref[:, 0:MLA_HEADS * MLA_V_DIM] = o_mla.astype(BF16)

    nr = FOX_HEADS * QBLK
    m_r, l_r, a_r = m_scr.at[0:nr], l_scr.at[0:nr], acc_scr.at[0:nr]
    _softmax_init(m_r, l_r, a_r)
    qp4 = rows_pos(FOX_HEADS)

    def fox_step(kb, diagonal):
        k = kvf_ref[0, pl.ds(pl.multiple_of(kb * QBLK, QBLK), QBLK), :]
        ck = cumr_ref[0, kb]
        ck4 = jnp.concatenate([jnp.broadcast_to(ck[h:h + 1, :], (QBLK, LANES)) for h in range(FOX_HEADS)], axis=0)
        s = _dot_t(fq_scr[...], k[:, 0:LANES]) + (cq_scr[...] - ck4)
        if diagonal:
            s = jnp.where(key_pos(FOX_HEADS, kb) <= qp4, s, NEG_INF)
        _softmax_update(m_r, l_r, a_r, s, k[:, LANES:2 * LANES])

    def fox_body(kb, c):
        fox_step(kb, False)
        return c

    lax.fori_loop(0, qi, fox_body, 0)
    fox_step(qi, True)
    o_fox = a_r[...] / l_r[...]
    base = MLA_HEADS * MLA_V_DIM
    for h in range(FOX_HEADS):
        mix_ref[:, base + h * LANES:base + (h + 1) * LANES] = o_fox[h * QBLK:(h + 1) * QBLK].astype(BF16)

    half = n_cmp // 2
    cmpk = jnp.concatenate([cmp_ref[0, pl.ds(0, half, stride=2), :], cmp_ref[0, pl.ds(1, half, stride=2), :]],
                           axis=0).astype(BF16)
    nqv = nq_scr[...]
    bias_c = jnp.concatenate([cmpb_ref[h, 0] for h in range(NSA_HEADS)], axis=0)
    lane_c = lax.broadcasted_iota(I32, (nr, n_cmp), 1)
    cmp_end = _cmp_block_of_lane(lane_c, half) * CMP_BLOCK + (CMP_BLOCK - 1)
    qpc = q0 + (lax.broadcasted_iota(I32, (nr, n_cmp), 0) & (QBLK - 1))
    valid_c = cmp_end <= qpc
    s = jnp.where(valid_c, _dot_t(nqv, cmpk) + bias_c, NEG_INF)
    p = jnp.exp(s - jnp.max(s, axis=1, keepdims=True))
    pc = jnp.where(valid_c, p / jnp.sum(p, axis=1, keepdims=True), 0.0)
    o_cmp = _dot(pc.astype(BF16), cmpk)

    pcs = pc[0:QBLK]
    for h in range(1, NSA_HEADS):
        pcs = pcs + pc[h * QBLK:(h + 1) * QBLK]
    imp = pcs[:, 0:half] + pcs[:, half:n_cmp]
    imp_t = jnp.concatenate([imp, jnp.zeros((QBLK, LANES - n_sel), F32)], axis=1).T[0:n_sel, :]
    blk = lax.broadcasted_iota(I32, (n_sel, QBLK), 0)
    qps = q0 + lax.broadcasted_iota(I32, (n_sel, QBLK), 1)
    forced = (blk == qps // SEL_BLOCK) | (blk == 0)
    imp_t = jnp.where(forced, FORCE_SCORE, imp_t)
    imp_t = jnp.where(blk * SEL_BLOCK > qps, -1.0, imp_t)
    rank = jnp.zeros((n_sel, QBLK), F32)
    for i in range(n_sel):
        row = imp_t[i:i + 1, :]
        rank = rank + jnp.where((row > imp_t) | ((row == imp_t) & (blk > i)), 1.0, 0.0)
    sel_t = jnp.where(rank < top, 1.0, 0.0)
    sel = jnp.concatenate([sel_t, jnp.zeros((LANES - n_sel, QBLK), F32)], axis=0).T
    selk = _dot(sel.astype(BF16), eblk_ref[...])
    for kb in range(n_kb):
        selk_scr[kb] = selk[:, kb * QBLK:(kb + 1) * QBLK]

    def toep4(off):
        return jnp.concatenate([toep_ref[h, off] for h in range(NSA_HEADS)], axis=0)

    _softmax_init(m_r, l_r, a_r)

    def sel_step(kb, c):
        k = kvn_ref[0, pl.ds(pl.multiple_of(kb * QBLK, QBLK), QBLK), :][:, LANES:2 * LANES]
        s = _dot_t(nq_scr[...], k) + toep4(qi - kb)
        mk = selk_scr[kb]
        mk4 = jnp.concatenate([mk] * NSA_HEADS, axis=0)
        s = jnp.where((key_pos(NSA_HEADS, kb) <= qp4) & (mk4 > 0.5), s, NEG_INF)
        _softmax_update(m_r, l_r, a_r, s, k)
        return c

    lax.fori_loop(0, qi + 1, sel_step, 0)
    o_sel = a_r[...] / l_r[...]

    _softmax_init(m_r, l_r, a_r)

    def win_step(kb, c):
        k = kvw_ref[0, pl.ds(pl.multiple_of(kb * QBLK, QBLK), QBLK), :]
        s = _dot_t(nq_scr[...], k) + toep4(qi - kb)
        dist = qp4 - key_pos(NSA_HEADS, kb)
        s = jnp.where((dist >= 0) & (dist <= WINDOW), s, NEG_INF)
        _softmax_update(m_r, l_r, a_r, s, k)
        return c

    lax.fori_loop(jnp.maximum(qi - WINDOW // QBLK, 0), qi + 1, win_step, 0)
    o_win = a_r[...] / l_r[...]

    misc = misc_ref[...]
    base = MLA_HEADS * MLA_V_DIM + FOX_HEADS * LANES
    for h in range(NSA_HEADS):
        rows = slice(h * QBLK, (h + 1) * QBLK)
        gate = lambda c: misc[:, _L_GATE + c * NSA_HEADS + h:_L_GATE + c * NSA_HEADS + h + 1]
        o = gate(0) * o_cmp[rows] + gate(1) * o_sel[rows] + gate(2) * o_win[rows]
        mix_ref[:, base + h * LANES:base + (h + 1) * LANES] = o.astype(BF16)


_MIX_PAD = MLA_HEADS * MLA_V_DIM + (FOX_HEADS + NSA_HEADS) * LANES


def _attn_prompt(po, cumc, cumr, toep, cmpb, wuv, b, s):
    nq = s // QBLK
    qblk = lambda w: pl.BlockSpec((QBLK, w), lambda bi, qi: (bi * nq + qi, 0))
    per_b = lambda a: pl.BlockSpec((1,) + a.shape[1:], lambda bi, qi: (bi,) + (0,) * (a.ndim - 1))
    const = lambda a: pl.BlockSpec(a.shape, lambda bi, qi: (0,) * a.ndim)
    kvm = po["kvm"].reshape(b, s, 2 * LANES)
    kvf = po["kvf"].reshape(b, s, 2 * LANES)
    kvn = po["kvn"].reshape(b, s, 2 * LANES)
    kvw = po["kvw"].reshape(b, s, LANES)
    cmpm = po["cmp"].reshape(b, s // CMP_BLOCK, LANES)
    rows = MLA_HEADS * QBLK
    assert s // SEL_BLOCK <= LANES
    eblk = (jnp.arange(s, dtype=I32)[None, :] // SEL_BLOCK == jnp.arange(LANES, dtype=I32)[:, None]).astype(BF16)
    return pl.pallas_call(
        _attn_prompt_kernel,
        grid=(b, nq),
        in_specs=[qblk(MLA_HEADS * 2 * LANES), qblk(FOX_HEADS * LANES), qblk(NSA_HEADS * LANES),
                  pl.BlockSpec((1, QBLK, LANES), lambda bi, qi: (bi, qi, 0)), qblk(LANES),
                  pl.BlockSpec((NSA_HEADS, 1, QBLK, s // CMP_BLOCK), lambda bi, qi: (0, qi, 0, 0)),
                  per_b(kvm), per_b(kvf), per_b(kvn), per_b(kvw), per_b(cmpm), per_b(cumr),
                  const(toep), const(wuv), const(eblk)],
        out_specs=qblk(_MIX_PAD),
        out_shape=jax.ShapeDtypeStruct((b * s, _MIX_PAD), BF16),
        scratch_shapes=[pltpu.VMEM((rows, 2 * LANES), BF16),
                        pltpu.VMEM((FOX_HEADS * QBLK, LANES), BF16),
                        pltpu.VMEM((NSA_HEADS * QBLK, LANES), BF16),
                        pltpu.VMEM((FOX_HEADS * QBLK, LANES), F32),
                        pltpu.VMEM((nq, QBLK, QBLK), F32),
                        pltpu.VMEM((rows, LANES), F32), pltpu.VMEM((rows, LANES), F32),
                        pltpu.VMEM((rows, LANES), F32)],
        compiler_params=_cparams(("parallel", "arbitrary"), 48),
        name="attn_prompt",
    )(po["qm"], po["fq"], po["nq"], cumc, po["misc"], cmpb, kvm, kvf, kvn, kvw, cmpm, cumr, toep, wuv, eblk)


def _wo_kernel(mix_ref, x_ref, gt_ref, w_ref, o_ref):
    o_ref[...] = x_ref[...] + gt_ref[0] * _dot(mix_ref[...], w_ref[...])


def _wo(mix, x, gt, w, tm):
    n = x.shape[0]
    nblk = n // tm
    bpg = nblk // gt.shape[0]
    return pl.pallas_call(
        _wo_kernel,
        grid=(nblk,),
        in_specs=[pl.BlockSpec((tm, mix.shape[1]), lambda i: (i, 0)),
                  pl.BlockSpec((tm, D_MODEL), lambda i: (i, 0)),
                  pl.BlockSpec((1,) + gt.shape[1:], lambda i: (i // bpg, 0, 0)),
                  pl.BlockSpec(w.shape, lambda i: (0, 0))],
        out_specs=pl.BlockSpec((tm, D_MODEL), lambda i: (i, 0)),
        out_shape=jax.ShapeDtypeStruct((n, D_MODEL), F32),
        compiler_params=_cparams(("parallel",), 40),
        name="out_proj",
    )(mix, x, gt, w)


def _final_norm_kernel(x_ref, g_ref, o_ref):
    o_ref[...] = _rms(x_ref[...], g_ref[...])


def _final_norm(x, g, tm):
    n = x.shape[0]
    return pl.pallas_call(
        _final_norm_kernel,
        grid=(n // tm,),
        in_specs=[pl.BlockSpec((tm, D_MODEL), lambda i: (i, 0)), pl.BlockSpec((1, D_MODEL), lambda i: (0, 0))],
        out_specs=pl.BlockSpec((tm, D_MODEL), lambda i: (i, 0)),
        out_shape=jax.ShapeDtypeStruct((n, D_MODEL), F32),
        compiler_params=_cparams(("parallel",)),
        name="final_norm",
    )(x, g)


_NEW_PAD = LANES


def _rows_from_tokens(tok, n_rows, per):
    r = lax.broadcasted_iota(I32, (n_rows, LANES), 0)
    out = jnp.zeros((n_rows, LANES), F32)
    for i in range(n_rows // per):
        out = jnp.where(r // per == i, jnp.broadcast_to(tok[i:i + 1, :], (n_rows, LANES)), out)
    return out


def _pick_lane(x, lane_of_row):
    lane = lax.broadcasted_iota(I32, x.shape, 1)
    return jnp.sum(jnp.where(lane == lane_of_row, x, 0.0), axis=1, keepdims=True)


def _decode1_kernel(pt_ref, qm_ref, fq_ref, newm_ref, newf_ref, cnew_ref, wuv_ref, *rest, pp, n_new):
    del pt_ref
    pages = rest[:4 * pp]
    omla_ref, ofox_ref, cmpm_ref = rest[4 * pp:4 * pp + 3]
    m1, l1, a1, m2, l2, a2, suf, newc = rest[4 * pp + 3:]
    jj = pl.program_id(1)
    r_m = MLA_HEADS * n_new
    r_f = FOX_HEADS * n_new
    row_f = lax.broadcasted_iota(I32, (r_f, LANES), 0)
    lane_f = lax.broadcasted_iota(I32, (r_f, LANES), 1)
    head_lane = _L_LOGF + (row_f & (FOX_HEADS - 1))

    @pl.when(jj == 0)
    def _():
        _softmax_init(m1, l1, a1)
        _softmax_init(m2, l2, a2)
        suf[...] = jnp.zeros(suf.shape, F32)
        x = cnew_ref[0]
        sub = lax.broadcasted_iota(I32, (SUBLANES, LANES), 0)
        y = x + jnp.where(sub >= 1, pltpu.roll(x, 1, 0), 0.0)
        y = y + jnp.where(sub >= 2, pltpu.roll(y, 2, 0), 0.0)
        newc[0:SUBLANES, :] = y
        col = _pick_lane(_rows_from_tokens(y, r_f, FOX_HEADS), head_lane)
        newc[SUBLANES:SUBLANES + r_f, :] = jnp.broadcast_to(col, (r_f, LANES))

    q = qm_ref[0]
    fq = fq_ref[0]
    ri = lax.broadcasted_iota(I32, (LANES, LANES), 0)
    ci_ = lax.broadcasted_iota(I32, (LANES, LANES), 1)
    later = jnp.where(ri > ci_, 1.0, 0.0).astype(BF16)
    for i in range(pp):
        pm, pf, plf, pn = pages[4 * i:4 * i + 4]
        rowm = pm[0, 0]
        c = rowm[:, 0:KV_RANK].astype(BF16)
        kr = rowm[:, KV_RANK:KV_RANK + ROPE_DIM].astype(BF16)
        s = _dot_t(q[:, 0:KV_RANK], c) + _dot_t(q[:, KV_RANK:KV_RANK + ROPE_DIM], kr)
        _softmax_update(m1, l1, a1, s, c)
        rowf = pf[0, 0]
        lf = plf[0, 0]
        x1, x2, x3 = _split3(jnp.concatenate([lf, lf], axis=0))
        exc = ((_dot(x1, later) + _dot(x2, later)) + _dot(x3, later))[0:SUBLANES]
        exc8 = exc + pltpu.roll(exc, FOX_HEADS, 0)
        tot8 = jnp.broadcast_to(exc[:, 0:1] + lf[:, 0:1], (SUBLANES, LANES))
        tot8 = tot8 + pltpu.roll(tot8, FOX_HEADS, 0)
        decay = jnp.concatenate([exc8] * (r_f // SUBLANES), axis=0) + suf[...] + newc[SUBLANES:SUBLANES + r_f, :]
        s = _dot_t(fq, rowf[:, 0:LANES].astype(BF16)) + decay
        _softmax_update(m2, l2, a2, s, rowf[:, LANES:2 * LANES].astype(BF16))
        suf[...] = suf[...] + jnp.concatenate([tot8] * (r_f // SUBLANES), axis=0)
        kc = pn[0, 0]
        cmpm_ref[0, pp - 1 - i] = jnp.sum(kc.reshape(PAGE_SIZE // CMP_BLOCK, CMP_BLOCK, LANES), axis=1) * (1.0 / CMP_BLOCK)

    @pl.when(jj == pl.num_programs(1) - 1)
    def _():
        lane_m = lax.broadcasted_iota(I32, (r_m, _NEW_PAD), 1)
        row_m = lax.broadcasted_iota(I32, (r_m, _NEW_PAD), 0)
        kn = newm_ref[0]
        s = jnp.where((lane_m <= row_m // MLA_HEADS) & (lane_m < n_new), _dot_t(q, kn), NEG_INF)
        _softmax_update(m1, l1, a1, s, kn[:, 0:KV_RANK])
        o_lat = (a1[...] / l1[...]).astype(BF16)
        rr = lax.broadcasted_iota(I32, (r_m, LANES), 0)
        o = jnp.zeros((r_m, MLA_HEADS * MLA_V_DIM), F32)
        for h in range(MLA_HEADS):
            o = o + _dot(jnp.where((rr & (MLA_HEADS - 1)) == h, o_lat, jnp.zeros_like(o_lat)), wuv_ref[h])
        omla_ref[0] = jnp.sum(o.reshape(n_new, MLA_HEADS, MLA_HEADS * MLA_V_DIM), axis=1)

        kf = newf_ref[0]
        yc = newc[0:SUBLANES, :]
        ci = newc[SUBLANES:SUBLANES + r_f, :]
        lane_n = lax.broadcasted_iota(I32, (r_f, _NEW_PAD), 1)
        row_n = lax.broadcasted_iota(I32, (r_f, _NEW_PAD), 0)
        decay = jnp.zeros((r_f, _NEW_PAD), F32)
        for j in range(n_new):
            cj = _pick_lane(jnp.broadcast_to(yc[j:j + 1, :], (r_f, LANES)), head_lane)
            decay = jnp.where(lane_n == j, ci[:, 0:_NEW_PAD] - cj, decay)
        s = _dot_t(fq, kf[:, 0:LANES]) + decay
        s = jnp.where((lane_n <= row_n // FOX_HEADS) & (lane_n < n_new), s, NEG_INF)
        _softmax_update(m2, l2, a2, s, kf[:, LANES:2 * LANES])
        ofox_ref[0] = a2[...] / l2[...]


def _decode1(layer, page_table, qm, fq, newm, newf, cnew, wuv, cache_mla, cache_fox, logf_t, cache_nsa, pp):
    b, n_pages = page_table.shape
    n_new = qm.shape[1] // MLA_HEADS
    steps = n_pages // pp
    per_b = lambda a: pl.BlockSpec((1,) + a.shape[1:], lambda bi, jj, pt: (bi,) + (0,) * (a.ndim - 1))

    def page_spec(width, lane_blk, i, rows=PAGE_SIZE):
        def imap(bi, jj, pt):
            return (layer, pt[bi, n_pages - 1 - (jj * pp + i)], 0, lane_blk)
        return pl.BlockSpec((1, 1, rows, width), imap)

    in_specs = [per_b(qm), per_b(fq), per_b(newm), per_b(newf), per_b(cnew),
                pl.BlockSpec(wuv.shape, lambda bi, jj, pt: (0, 0, 0))]
    args = [qm, fq, newm, newf, cnew, wuv]
    for i in range(pp):
        in_specs += [page_spec(KV_RANK + ROPE_DIM, 0, i), page_spec(2 * LANES, 0, i),
                     page_spec(LANES, 0, i, rows=SUBLANES), page_spec(LANES, 0, i)]
        args += [cache_mla, cache_fox, logf_t, cache_nsa]
    r_m, r_f = MLA_HEADS * n_new, FOX_HEADS * n_new
    grid_spec = pltpu.PrefetchScalarGridSpec(
        num_scalar_prefetch=1, grid=(b, steps), in_specs=in_specs,
        out_specs=[pl.BlockSpec((1, n_new, MLA_HEADS * MLA_V_DIM), lambda bi, jj, pt: (bi, 0, 0)),
                   pl.BlockSpec((1, r_f, LANES), lambda bi, jj, pt: (bi, 0, 0)),
                   pl.BlockSpec((1, pp, PAGE_SIZE // CMP_BLOCK, LANES), lambda bi, jj, pt: (bi, steps - 1 - jj, 0, 0))],
        scratch_shapes=[pltpu.VMEM((r_m, LANES), F32), pltpu.VMEM((r_m, LANES), F32), pltpu.VMEM((r_m, LANES), F32),
                        pltpu.VMEM((r_f, LANES), F32), pltpu.VMEM((r_f, LANES), F32), pltpu.VMEM((r_f, LANES), F32),
                        pltpu.VMEM((r_f, LANES), F32), pltpu.VMEM((SUBLANES + r_f, LANES), F32)])
    return pl.pallas_call(
        functools.partial(_decode1_kernel, pp=pp, n_new=n_new),
        grid_spec=grid_spec,
        out_shape=[jax.ShapeDtypeStruct((b, n_new, MLA_HEADS * MLA_V_DIM), F32),
                   jax.ShapeDtypeStruct((b, r_f, LANES), F32),
                   jax.ShapeDtypeStruct((b, n_pages, PAGE_SIZE // CMP_BLOCK, LANES), F32)],
        compiler_params=_cparams(("parallel", "arbitrary"), 40),
        name="decode_mla_fox",
    )(page_table, *args)


def _decode2_kernel(pt_ref, nq_ref, misc_ref, cmp_ref, bkey_ref, bnear_ref, bcmp_ref, win_ref, neww_ref, newn_ref,
                    *rest, pp, n_new):
    del pt_ref
    pages = rest[:pp]
    onsa_ref = rest[pp]
    selk, ocmp, m, l, acc = rest[pp + 1:]
    jj = pl.program_id(1)
    n_pages = selk.shape[0]
    r_n = NSA_HEADS * n_new
    nq = nq_ref[0]
    row = lax.broadcasted_iota(I32, (r_n, LANES), 0)
    lane = lax.broadcasted_iota(I32, (r_n, LANES), 1)

    @pl.when(jj == 0)
    def _():
        cmpk = cmp_ref[0].astype(BF16)
        s = _dot_t(nq, cmpk) + bcmp_ref[...]
        p = jnp.exp(s - jnp.max(s, axis=1, keepdims=True))
        pc = p / jnp.sum(p, axis=1, keepdims=True)
        ocmp[...] = _dot(pc.astype(BF16), cmpk)
        gr = lax.broadcasted_iota(I32, (r_n, r_n), 0) // NSA_HEADS
        gc = lax.broadcasted_iota(I32, (r_n, r_n), 1) // NSA_HEADS
        same_q = jnp.where(gr == gc, 1.0, 0.0).astype(BF16)
        p1, p2, p3 = _split3(pc)
        pcs = (_dot(same_q, p1) + _dot(same_q, p2)) + _dot(same_q, p3)
        half = pcs.shape[1] // 2
        imp = pcs[:, 0:half] + pcs[:, half:2 * half]
        lane_b = lax.broadcasted_iota(I32, imp.shape, 1)
        imp = jnp.where(lane_b == 0, FORCE_SCORE, imp)
        n_sel = half + 1
        sel = _topk_mask_lanes(imp, lane_b.astype(F32), min(SEL_TOPK, n_sel) - 1)
        per_pg = PAGE_SIZE // SEL_BLOCK
        for pg in range(n_pages):
            mk = jnp.zeros((r_n, LANES), F32)
            for u in range(per_pg):
                col = jnp.broadcast_to(sel[:, pg * per_pg + u:pg * per_pg + u + 1], (r_n, LANES))
                mk = jnp.where(lane // SEL_BLOCK == u, col, mk)
            selk[pg] = mk
        _softmax_init(m, l, acc)

    for i in range(pp):
        pg = jj * pp + i
        k = pages[i][0, 0].astype(BF16)
        s = _dot_t(nq, k) + bkey_ref[:, i * LANES:(i + 1) * LANES]
        s = jnp.where(selk[pg] > 0.5, s, NEG_INF)
        _softmax_update(m, l, acc, s, k)

    @pl.when(jj == pl.num_programs(1) - 1)
    def _():
        lane_n = lax.broadcasted_iota(I32, (r_n, _NEW_PAD), 1)
        row_n = lax.broadcasted_iota(I32, (r_n, _NEW_PAD), 0)
        own = (lane_n <= row_n // NSA_HEADS) & (lane_n < n_new)
        bnew = bnear_ref[:, WINDOW:WINDOW + LANES][:, 0:_NEW_PAD]
        kn = newn_ref[0]
        s = jnp.where(own, _dot_t(nq, kn) + bnew, NEG_INF)
        _softmax_update(m, l, acc, s, kn)
        o_sel = acc[...] / l[...]
        _softmax_init(m, l, acc)
        for c in range(WINDOW // LANES):
            kw = win_ref[0, c * LANES:(c + 1) * LANES, :].astype(BF16)
            s = _dot_t(nq, kw) + bnear_ref[:, c * LANES:(c + 1) * LANES]
            if c == 0:
                s = jnp.where(lane >= row // NSA_HEADS, s, NEG_INF)
            _softmax_update(m, l, acc, s, kw)
        kwn = neww_ref[0]
        s = jnp.where(own, _dot_t(nq, kwn) + bnew, NEG_INF)
        _softmax_update(m, l, acc, s, kwn)
        o_win = acc[...] / l[...]
        g16 = _rows_from_tokens(misc_ref[0], r_n, NSA_HEADS)
        gate = lambda c: _pick_lane(g16, _L_GATE + c * NSA_HEADS + (row & (NSA_HEADS - 1)))
        onsa_ref[0] = gate(0) * ocmp[...] + gate(1) * o_sel + gate(2) * o_win


def _decode2(layer, page_table, nq, misc, cmp_eo, bkey, bnear, bcmp, win, neww, newn, cache_nsa, pp):
    b, n_pages = page_table.shape
    r_n = nq.shape[1]
    n_new = r_n // NSA_HEADS
    steps = n_pages // pp
    per_b = lambda a: pl.BlockSpec((1,) + a.shape[1:], lambda bi, jj, pt: (bi,) + (0,) * (a.ndim - 1))
    const = lambda a: pl.BlockSpec(a.shape, lambda bi, jj, pt: (0,) * a.ndim)
    in_specs = [per_b(nq), per_b(misc), per_b(cmp_eo),
                pl.BlockSpec((r_n, pp * LANES), lambda bi, jj, pt: (0, jj)), const(bnear), const(bcmp),
                per_b(win), per_b(neww), per_b(newn)]
    args = [nq, misc, cmp_eo, bkey, bnear, bcmp, win, neww, newn]
    for i in range(pp):
        in_specs.append(pl.BlockSpec((1, 1, PAGE_SIZE, LANES),
                                     lambda bi, jj, pt, i=i: (layer, pt[bi, jj * pp + i], 0, 1)))
        args.append(cache_nsa)
    grid_spec = pltpu.PrefetchScalarGridSpec(
        num_scalar_prefetch=1, grid=(b, steps), in_specs=in_specs,
        out_specs=pl.BlockSpec((1, r_n, LANES), lambda bi, jj, pt: (bi, 0, 0)),
        scratch_shapes=[pltpu.VMEM((n_pages, r_n, LANES), F32), pltpu.VMEM((r_n, LANES), F32),
                        pltpu.VMEM((r_n, LANES), F32), pltpu.VMEM((r_n, LANES), F32), pltpu.VMEM((r_n, LANES), F32)])
    return pl.pallas_call(
        functools.partial(_decode2_kernel, pp=pp, n_new=n_new),
        grid_spec=grid_spec,
        out_shape=jax.ShapeDtypeStruct((b, r_n, LANES), F32),
        compiler_params=_cparams(("parallel", "arbitrary"), 40),
        name="decode_nsa",
    )(page_table, *args)


def _rope_swap(w):
    half = ROPE_DIM // 2
    return jnp.concatenate([w[..., half:], w[..., :half]], axis=-1)


def _pad_lanes(w, width=LANES, at=0):
    out = jnp.zeros(w.shape[:-1] + (width,), w.dtype)
    return out.at[..., at:at + w.shape[-1]].set(w)


def _prep_layer(l, w_in, b_f, g_q, g_kv, g_mix, w_uq, w_uk, w_uv, w_o):
    offs = np.concatenate([[0], np.cumsum(IN_SPLITS)])
    col = lambda i: w_in[l][:, offs[i]:offs[i + 1]]
    cq, ckv, kr, fq, fk, fv, ff, nq, kc, vc, ks, vs, kw, vw, ng = [col(i) for i in range(len(IN_SPLITS))]
    ga = _pad_lanes(jnp.concatenate([kr, ff, ng], axis=1))
    gb = _pad_lanes(_rope_swap(kr))
    fq4 = [_pad_lanes(fq[:, h * HEAD_DIM:(h + 1) * HEAD_DIM], at=(h // FOX_GROUP) * HEAD_DIM) for h in range(FOX_HEADS)]
    nq4 = [_pad_lanes(nq[:, h * HEAD_DIM:(h + 1) * HEAD_DIM]) for h in range(NSA_HEADS)]
    w1 = jnp.concatenate([cq, ckv, ga, gb] + fq4 + [fk, fv] + nq4 + [kc, vc, ks, vs, kw, vw], axis=1).astype(BF16)
    uq = w_uq[l]
    nope = uq[:, :, :NOPE_DIM].reshape(Q_RANK, MLA_HEADS * NOPE_DIM)
    rope = _pad_lanes(uq[:, :, NOPE_DIM:]).reshape(Q_RANK, MLA_HEADS * LANES)
    rsw = _pad_lanes(_rope_swap(uq[:, :, NOPE_DIM:])).reshape(Q_RANK, MLA_HEADS * LANES)
    wq = jnp.concatenate([nope, rope, rsw], axis=1).astype(BF16)
    wuk = jnp.zeros((MLA_HEADS * NOPE_DIM, MLA_HEADS * KV_RANK), F32)
    wuv = jnp.zeros((MLA_HEADS, KV_RANK, MLA_HEADS * MLA_V_DIM), F32)
    for h in range(MLA_HEADS):
        wuk = wuk.at[h * NOPE_DIM:(h + 1) * NOPE_DIM, h * KV_RANK:(h + 1) * KV_RANK].set(w_uk[l][:, h, :].T)
        wuv = wuv.at[h, :, h * MLA_V_DIM:(h + 1) * MLA_V_DIM].set(w_uv[l][:, h, :])
    wo = jnp.zeros((_MIX_PAD, D_MODEL), F32)
    n_mla = MLA_HEADS * MLA_V_DIM
    wo = wo.at[0:n_mla].set(w_o[l][0:n_mla])
    for h in range(FOX_HEADS):
        r0 = n_mla + h * LANES + (h // FOX_GROUP) * HEAD_DIM
        wo = wo.at[r0:r0 + HEAD_DIM].set(w_o[l][n_mla + h * HEAD_DIM:n_mla + (h + 1) * HEAD_DIM])
    for h in range(NSA_HEADS):
        r0 = n_mla + (FOX_HEADS + h) * LANES + HEAD_DIM
        src = n_mla + (FOX_HEADS + h) * HEAD_DIM
        wo = wo.at[r0:r0 + HEAD_DIM].set(w_o[l][src:src + HEAD_DIM])
    return dict(w1=w1, wq=wq, wuk=wuk.astype(BF16), wuv=wuv.astype(BF16), wo=wo.astype(BF16),
                g_mix=g_mix[l][None], g_q=g_q[l][None], g_kv=g_kv[l][None],
                bf=_pad_lanes(b_f[l][None], at=_L_LOGF))


def _rope_tables(pos):
    half = ROPE_DIM // 2
    inv = ROPE_BASE ** (-jnp.arange(half, dtype=F32) / half)
    ang = pos.astype(F32)[:, None] * inv
    cos, sin = jnp.cos(ang), jnp.sin(ang)
    return _pad_lanes(jnp.concatenate([cos, cos], axis=1)), _pad_lanes(jnp.concatenate([-sin, sin], axis=1))


_PROJ_NAMES = ("mla", "fkv", "nkv", "win", "misc", "qm", "kvm", "fq", "kvf", "nq", "kvn", "kvw", "cmp")


def _pad_rows(a, rows):
    return jnp.pad(a, ((0, 0), (0, rows - a.shape[1]), (0, 0)))


def kernel(x_prompt, x_sample, c_prompt, c_sample, cache_mla, cache_fox_kv, cache_fox_logf, cache_nsa_kv, state_nsa_win, page_table, w_ada, b_ada, g_mix, g_ffn, g_final, w_in, b_f, g_q, g_kv, w_uq, w_uk, w_uv, w_o, t5_table, w_pq, sub_keys, expert_u, expert_v):
    bp, sp, d = x_prompt.shape
    bs, ss, _ = x_sample.shape
    depth = w_in.shape[0]
    n_pool = cache_mla.shape[1]
    n_pages = page_table.shape[1]
    past_len = n_pages * PAGE_SIZE
    assert state_nsa_win.shape[2] == WINDOW and sp % QBLK == 0 and sp > QBLK
    np_, ns_ = bp * sp, bs * ss
    tm_p = 256
    tm_s = min(256, ns_)
    pp = next(p for p in (8, 4, 2, 1) if n_pages % p == 0)

    xp = x_prompt.reshape(np_, d)
    xs = x_sample.reshape(ns_, d)
    cache_fox = cache_fox_kv.reshape(depth, n_pool, PAGE_SIZE, 2 * FOX_KV_HEADS * HEAD_DIM)
    cache_nsa = cache_nsa_kv.reshape(depth, n_pool, PAGE_SIZE, 4 * HEAD_DIM)
    logf_t = _pad_rows(cache_fox_logf.transpose(0, 1, 3, 2).reshape(depth * n_pool, FOX_HEADS, PAGE_SIZE),
                       SUBLANES).reshape(depth, n_pool, SUBLANES, PAGE_SIZE)
    win_buf = state_nsa_win.reshape(depth, bs, WINDOW, 2 * HEAD_DIM)

    cm_p, sm_p = _rope_tables(jnp.arange(sp, dtype=I32))
    cm_s, sm_s = _rope_tables(past_len + jnp.arange(ss, dtype=I32))
    cm_s, sm_s = jnp.tile(cm_s, (tm_s // ss, 1)), jnp.tile(sm_s, (tm_s // ss, 1))
    toep, cmpb = _t5_prompt(t5_table, sp)
    bkey, bnear, bcmp = _t5_decode(t5_table, past_len, ss, WINDOW)
    rows_ih = lambda t: t[:, :ss].transpose(1, 0, 2).reshape(ss * NSA_HEADS, t.shape[2])
    bkey, bnear, bcmp = rows_ih(bkey), rows_ih(bnear), rows_ih(bcmp)

    c_all = jnp.concatenate([c_prompt, c_sample], axis=0)
    st_p, st_s = [], []
    for l in range(depth):
        lw = _prep_layer(l, w_in, b_f, g_q, g_kv, g_mix, w_uq, w_uk, w_uv, w_o)
        mod = _ada(c_all, w_ada[l], b_ada[l][None])
        mod_p = [m[:, None, :] for m in jnp.split(mod[:bp], 6, axis=-1)]
        mod_s = [jnp.repeat(m, ss, axis=0).reshape(ns_ // tm_s, tm_s, d) for m in jnp.split(mod[bp:], 6, axis=-1)]
        wpq = w_pq[l].astype(BF16)
        keys = sub_keys[l].reshape(PEER_HEADS * 2, N_KEYS, PEER_KEY_DIM).astype(BF16)
        tbl_u, tbl_v = _pack_table(expert_u[l]), _pack_table(expert_v[l])

        po = dict(zip(_PROJ_NAMES, _proj(xp, mod_p[1], mod_p[0], lw, cm_p, sm_p, tm_p)))
        cumc, cumr = _cum(po["misc"].reshape(bp, sp, LANES))
        mix = _attn_prompt(po, cumc, cumr, toep, cmpb, lw["wuv"], bp, sp)
        xp = _wo(mix, xp, mod_p[2], lw["wo"], tm_p)
        xp = _peer_block(xp, mod_p[4], mod_p[3], mod_p[5], g_ffn[l][None], wpq, keys, tbl_u, tbl_v, tm_p, 128)
        w_keep = min(WINDOW, sp)
        st_p.append((po["mla"].reshape(bp, sp, -1),
                     po["fkv"].reshape(bp, sp, 2, FOX_KV_HEADS, HEAD_DIM),
                     po["misc"][:, _L_LOGF:_L_GATE].reshape(bp, sp, FOX_HEADS),
                     po["nkv"].reshape(bp, sp, 4, HEAD_DIM),
                     po["win"].reshape(bp, sp, 2, HEAD_DIM)[:, sp - w_keep:]))

        so = dict(zip(_PROJ_NAMES, _proj(xs, mod_s[1], mod_s[0], lw, cm_s, sm_s, tm_s)))
        per_tok = lambda a: a.reshape(bs, ss, a.shape[1])
        newk = lambda a: _pad_rows(per_tok(a), _NEW_PAD)
        qm = so["qm"].reshape(bs, ss * MLA_HEADS, 2 * LANES)
        fq = so["fq"].reshape(bs, ss * FOX_HEADS, LANES)
        nq = so["nq"].reshape(bs, ss * NSA_HEADS, LANES)
        misc8 = _pad_rows(per_tok(so["misc"]), SUBLANES)
        omla, ofox, cmpm = _decode1(l, page_table, qm, fq, newk(so["kvm"]), newk(so["kvf"]), misc8, lw["wuv"],
                                    cache_mla, cache_fox, logf_t, cache_nsa, pp)
        cmpm = cmpm.reshape(bs, n_pages * (PAGE_SIZE // CMP_BLOCK), LANES)
        cmp_eo = jnp.concatenate([cmpm[:, 0::2], cmpm[:, 1::2]], axis=1)
        onsa = _decode2(l, page_table, nq, misc8, cmp_eo, bkey, bnear, bcmp, win_buf[l],
                        newk(so["kvw"]), newk(so["kvn"][:, 2 * HEAD_DIM:]), cache_nsa, pp)
        mix_s = jnp.concatenate([omla.reshape(ns_, -1), ofox.reshape(ns_, -1), onsa.reshape(ns_, -1)],
                                axis=1).astype(BF16)
        xs = _wo(mix_s, xs, mod_s[2], lw["wo"], tm_s)
        xs = _peer_block(xs, mod_s[4], mod_s[3], mod_s[5], g_ffn[l][None], wpq, keys, tbl_u, tbl_v, tm_s, 128)
        win_new = so["win"].reshape(bs, ss, 2, HEAD_DIM)
        win_all = jnp.concatenate([state_nsa_win[l], win_new], axis=1)
        st_s.append((per_tok(so["mla"]),
                     so["fkv"].reshape(bs, ss, 2, FOX_KV_HEADS, HEAD_DIM),
                     so["misc"][:, _L_LOGF:_L_GATE].reshape(bs, ss, FOX_HEADS),
                     so["nkv"].reshape(bs, ss, 4, HEAD_DIM),
                     win_all[:, win_all.shape[1] - min(WINDOW, win_all.shape[1]):]))

    yp = _final_norm(xp, g_final[None], tm_p).reshape(bp, sp, d)
    ys = _final_norm(xs, g_final[None], tm_s).reshape(bs, ss, d)
    stack = lambda st, i: jnp.stack([s[i] for s in st])
    return (yp, ys,
            stack(st_p, 0), stack(st_s, 0), stack(st_p, 1), stack(st_s, 1), stack(st_p, 2), stack(st_s, 2),
            stack(st_p, 3), stack(st_s, 3), stack(st_p, 4), stack(st_s, 4))
```

```python
import functools
import math

import numpy as np
import jax
import jax.numpy as jnp
from jax import lax
from jax.experimental import pallas as pl
from jax.experimental.pallas import tpu as pltpu

F32 = jnp.float32
BF16 = jnp.bfloat16
I32 = jnp.int32

D_MODEL = 1024
PAGE_SIZE = 128
HEAD_DIM = 64
MLA_HEADS = 8
Q_RANK = 256
KV_RANK = 128
ROPE_DIM = 32
NOPE_DIM = 64
MLA_V_DIM = 64
ROPE_BASE = 10000.0
FOX_HEADS = 4
FOX_KV_HEADS = 2
FOX_GROUP = FOX_HEADS // FOX_KV_HEADS
NSA_HEADS = 4
CMP_BLOCK = 32
SEL_BLOCK = 64
SEL_TOPK = 16
WINDOW = 512
FORCE_SCORE = 1000.0
N_BUCKETS = 32
MAX_DISTANCE = 1024
N_KEYS = 128
N_EXPERTS = N_KEYS * N_KEYS
PEER_HEADS = 8
PEER_TOPK = 16
PEER_KEY_DIM = 128
PEER_PICKS = PEER_HEADS * PEER_TOPK
QBLK = 128
EPS = 1e-6
NEG_INF = -1e30
POS_PAD = 2 ** 30
MLA_SCALE = (NOPE_DIM + ROPE_DIM) ** -0.5
HD_SCALE = HEAD_DIM ** -0.5
MIX_WIDTH = MLA_HEADS * MLA_V_DIM + FOX_HEADS * HEAD_DIM + NSA_HEADS * HEAD_DIM
IN_SPLITS = (Q_RANK, KV_RANK, ROPE_DIM,
             FOX_HEADS * HEAD_DIM, FOX_KV_HEADS * HEAD_DIM, FOX_KV_HEADS * HEAD_DIM, FOX_HEADS,
             NSA_HEADS * HEAD_DIM, HEAD_DIM, HEAD_DIM, HEAD_DIM, HEAD_DIM, HEAD_DIM, HEAD_DIM, 3 * NSA_HEADS)
IN_WIDTH = sum(IN_SPLITS)

SUBLANES = 8
LANES = 128
VMEM_BYTES_V7X = 64 * 1024 * 1024
HALF_EXPERTS = N_EXPERTS // 2


def _cparams(sem, vmem_mb=None):
    kw = dict(dimension_semantics=sem)
    if vmem_mb is not None:
        kw["vmem_limit_bytes"] = vmem_mb * 1024 * 1024
    return pltpu.CompilerParams(**kw)


def _topk_axis0(s, iota, k, payload=None):
    vals, outs = [], []
    for _ in range(k):
        m = jnp.max(s, axis=0, keepdims=True)
        first = jnp.min(jnp.where(s == m, iota, 1e9), axis=0, keepdims=True)
        onehot = iota == first
        if payload is None:
            outs.append(first)
        else:
            outs.append(jnp.max(jnp.where(onehot, payload, -1.0), axis=0, keepdims=True))
        s = jnp.where(onehot, -jnp.inf, s)
        vals.append(m)
    return vals, outs


_CAND_ROWS = ((0, PEER_TOPK),) + tuple((a, SUBLANES) for a in range(1, SUBLANES))
_N_CAND = PEER_TOPK + (SUBLANES - 1) * SUBLANES + SUBLANES


def _peer_route_kernel(x_ref, sc_ref, sh_ref, g_ref, wpq_ref, keys_ref,
                       h_ref, idx_ref, hi_ref, gate_ref):
    x = x_ref[...]
    y = x * lax.rsqrt(jnp.mean(x * x, axis=-1, keepdims=True) + EPS) * g_ref[...]
    h = y * (1.0 + sc_ref[0]) + sh_ref[0]
    h_ref[...] = h
    q = jnp.dot(h.astype(BF16), wpq_ref[...], preferred_element_type=F32).astype(BF16)
    tm = x.shape[0]
    iota_k = lax.broadcasted_iota(I32, (N_KEYS, tm), 0).astype(F32)
    rc = lax.broadcasted_iota(I32, (_N_CAND, tm), 0)
    mid = rc - PEER_TOPK
    flat = jnp.where(rc < PEER_TOPK, rc,
                     jnp.where(rc < _N_CAND - SUBLANES,
                               (1 + mid // SUBLANES) * PEER_TOPK + (mid & (SUBLANES - 1)),
                               (rc - (_N_CAND - 2 * SUBLANES)) * PEER_TOPK))
    iota_c = flat.astype(F32)
    for head in range(PEER_HEADS):
        tops = []
        for p in range(2):
            c = (head * 2 + p) * PEER_KEY_DIM
            s = lax.dot_general(keys_ref[head * 2 + p], q[:, c:c + PEER_KEY_DIM],
                                (((1,), (1,)), ((), ())), preferred_element_type=F32)
            tops.append(_topk_axis0(s, iota_k, PEER_TOPK))
        (v1, i1), (v2, i2) = tops
        v1s, i1s = jnp.concatenate(v1, axis=0), jnp.concatenate(i1, axis=0)
        v2s, i2s = jnp.concatenate(v2, axis=0), jnp.concatenate(i2, axis=0)
        cand = jnp.concatenate([v1[a] + v2s[0:nb] for a, nb in _CAND_ROWS] + [v1s[SUBLANES:] + v2[0]], axis=0)
        cidx = jnp.concatenate([i1[a] * float(N_KEYS) + i2s[0:nb] for a, nb in _CAND_ROWS]
                               + [i1s[SUBLANES:] * float(N_KEYS) + i2[0]], axis=0)
        tv, te = _topk_axis0(cand, iota_c, PEER_TOPK, payload=cidx)
        tv = jnp.concatenate(tv, axis=0)
        te = jnp.concatenate(te, axis=0).astype(I32)
        e = jnp.exp(tv - tv[0:1])
        gate = e / jnp.sum(e, axis=0, keepdims=True)
        rows = slice(head * PEER_TOPK, (head + 1) * PEER_TOPK)
        idx_ref[0, rows, :] = (te & (HALF_EXPERTS - 1)) * SUBLANES
        hi_ref[0, rows, :] = jnp.where(te >= HALF_EXPERTS, 1.0, 0.0)
        gate_ref[0, rows, :] = gate


def _peer_route(x, sc, sh, g, wpq, keys, tm):
    n = x.shape[0]
    nblk = n // tm
    bpg = nblk // sc.shape[0]
    mod_spec = pl.BlockSpec((1,) + sc.shape[1:], lambda i: (i // bpg, 0, 0))
    pick_spec = pl.BlockSpec((1, PEER_PICKS, tm), lambda i: (i, 0, 0))
    return pl.pallas_call(
        _peer_route_kernel,
        grid=(nblk,),
        in_specs=[pl.BlockSpec((tm, D_MODEL), lambda i: (i, 0)), mod_spec, mod_spec,
                  pl.BlockSpec((1, D_MODEL), lambda i: (0, 0)),
                  pl.BlockSpec(wpq.shape, lambda i: (0, 0)),
                  pl.BlockSpec(keys.shape, lambda i: (0, 0, 0))],
        out_specs=[pl.BlockSpec((tm, D_MODEL), lambda i: (i, 0)), pick_spec, pick_spec, pick_spec],
        out_shape=[jax.ShapeDtypeStruct((n, D_MODEL), F32),
                   jax.ShapeDtypeStruct((nblk, PEER_PICKS, tm), I32),
                   jax.ShapeDtypeStruct((nblk, PEER_PICKS, tm), F32),
                   jax.ShapeDtypeStruct((nblk, PEER_PICKS, tm), F32)],
        compiler_params=_cparams(("parallel",), 48),
        name="peer_route",
    )(x, sc, sh, g, wpq, keys)


_ROWS_PER_PICK = 2 * SUBLANES
PICK_ROWS = PEER_PICKS * _ROWS_PER_PICK


def _gelu_exact(x):
    return 0.5 * x * (1.0 + lax.erf(x * (2.0 ** -0.5)))


def _pick_expand():
    p = lax.broadcasted_iota(I32, (PEER_PICKS, PICK_ROWS), 0)
    k = lax.broadcasted_iota(I32, (PEER_PICKS, PICK_ROWS), 1)
    return jnp.where(k // _ROWS_PER_PICK == p, 1.0, 0.0).astype(BF16)


def _half_matches(hi_ref, expand):
    tm = hi_ref.shape[0]
    hi_cols = _dot(hi_ref[...].astype(BF16), expand)
    k = lax.broadcasted_iota(I32, (tm, PICK_ROWS), 1)
    return hi_cols == (k & 1).astype(F32)


def _chunk_diag(rows):
    s = lax.broadcasted_iota(I32, (rows, PICK_ROWS), 0)
    k = lax.broadcasted_iota(I32, (rows, PICK_ROWS), 1)
    return s == (k % _ROWS_PER_PICK) // 2


def _gather_rows(idx_ref, tbl_ref, g_scr, t):
    base = t * PEER_PICKS
    for p in range(PEER_PICKS):
        r = pl.multiple_of(idx_ref[base + p], SUBLANES)
        g_scr[p * SUBLANES:(p + 1) * SUBLANES, :] = tbl_ref[pl.ds(r, SUBLANES), :]


def _two_stage_tokens(tm, gather, compute, g0, g1):
    gather(g0, 0)

    def pair(i, c):
        t0 = 2 * i
        gather(g1, t0 + 1)
        compute(g0, t0)
        gather(g0, jnp.minimum(t0 + 2, tm - 1))
        compute(g1, t0 + 1)
        return c

    lax.fori_loop(0, tm // 2, pair, 0)


def _peer_act_kernel(idx_ref, hi_ref, h_ref, gate_ref, tbl_ref, w_ref, g0, g1, rs_scr):
    tm = h_ref.shape[0]
    diag = _chunk_diag(SUBLANES)

    def compute(g_scr, t):
        vt = pltpu.bitcast(g_scr[...], BF16)
        hb = h_ref[t].astype(BF16)
        r = _dot_t(jnp.concatenate([hb, hb], axis=0), vt)[0:SUBLANES]
        rs_scr[pl.ds(t, 1), :] = jnp.sum(jnp.where(diag, r, 0.0), axis=0, keepdims=True)

    _two_stage_tokens(tm, functools.partial(_gather_rows, idx_ref, tbl_ref), compute, g0, g1)
    expand = _pick_expand()
    x1, x2, x3 = _split3(jnp.where(_half_matches(hi_ref, expand), rs_scr[...], 0.0))
    act = (_dot_t(x1, expand) + _dot_t(x2, expand)) + _dot_t(x3, expand)
    w_ref[...] = gate_ref[...] * _gelu_exact(act)


def _peer_out_kernel(idx_ref, hi_ref, w_ref, x_ref, gt_ref, tbl_ref, o_ref, g0, g1, ws_scr):
    tm = x_ref.shape[0]
    expand = _pick_expand()
    w_cols = _dot(w_ref[...].astype(BF16), expand)
    ws_scr[...] = jnp.where(_half_matches(hi_ref, expand), w_cols, 0.0)
    diag = _chunk_diag(_ROWS_PER_PICK)

    def compute(g_scr, t):
        vt = pltpu.bitcast(g_scr[...], BF16)
        wrow = jnp.broadcast_to(ws_scr[pl.ds(t, 1), :], (_ROWS_PER_PICK, PICK_ROWS))
        o_ref[t] = _dot(jnp.where(diag, wrow, 0.0).astype(BF16), vt)[0:SUBLANES]

    _two_stage_tokens(tm, functools.partial(_gather_rows, idx_ref, tbl_ref), compute, g0, g1)
    o_ref[...] = x_ref[...] + gt_ref[0] * o_ref[...]


def _idx_spec(tm):
    return pl.BlockSpec((tm * PEER_PICKS,), lambda i: (i,), memory_space=pltpu.SMEM)


def _table_spec():
    return pl.BlockSpec((HALF_EXPERTS * SUBLANES, LANES), lambda i: (0, 0), pipeline_mode=pl.Buffered(1))


def _gather_scratch():
    return [pltpu.VMEM((PEER_PICKS * SUBLANES, LANES), I32), pltpu.VMEM((PEER_PICKS * SUBLANES, LANES), I32)]


def _peer_act(idx, hi, h3, gate, tbl, tm):
    n = h3.shape[0]
    picks = pl.BlockSpec((tm, PEER_PICKS), lambda i: (i, 0))
    return pl.pallas_call(
        _peer_act_kernel,
        grid=(n // tm,),
        in_specs=[_idx_spec(tm), picks, pl.BlockSpec((tm, SUBLANES, LANES), lambda i: (i, 0, 0)), picks,
                  _table_spec()],
        out_specs=picks,
        out_shape=jax.ShapeDtypeStruct((n, PEER_PICKS), F32),
        scratch_shapes=_gather_scratch() + [pltpu.VMEM((tm, PICK_ROWS), F32)],
        compiler_params=_cparams(("parallel",), 52),
        name="peer_act",
    )(idx, hi, h3, gate, tbl)


def _peer_out(idx, hi, w, x3, gt3, tbl, tm):
    n = x3.shape[0]
    bpg = (n // tm) // gt3.shape[0]
    tok = pl.BlockSpec((tm, SUBLANES, LANES), lambda i: (i, 0, 0))
    picks = pl.BlockSpec((tm, PEER_PICKS), lambda i: (i, 0))
    gt_spec = pl.BlockSpec((1,) + gt3.shape[1:], lambda i: (i // bpg, 0, 0, 0))
    return pl.pallas_call(
        _peer_out_kernel,
        grid=(n // tm,),
        in_specs=[_idx_spec(tm), picks, picks, tok, gt_spec, _table_spec()],
        out_specs=tok,
        out_shape=jax.ShapeDtypeStruct((n, SUBLANES, LANES), F32),
        scratch_shapes=_gather_scratch() + [pltpu.VMEM((tm, PICK_ROWS), F32)],
        compiler_params=_cparams(("parallel",), 52),
        name="peer_out",
    )(idx, hi, w, x3, gt3, tbl)


def _pack_table(t):
    b = lax.bitcast_convert_type(t.astype(BF16), jnp.uint16).astype(jnp.uint32)
    packed = b[:HALF_EXPERTS] | (b[HALF_EXPERTS:] << 16)
    return lax.bitcast_convert_type(packed, I32).reshape(HALF_EXPERTS * SUBLANES, LANES)


def _picks_token_major(a):
    nblk, p, tm = a.shape
    return a.transpose(0, 2, 1).reshape(nblk * tm, p)


def _peer_block(x, sc2, sh2, gt2, g_f, wpq_bf, keys_bf, tbl_u, tbl_v, tm_route, tm_pass):
    n = x.shape[0]
    h, idx, hi, gate = _peer_route(x, sc2, sh2, g_f, wpq_bf, keys_bf, tm_route)
    idx = _picks_token_major(idx).reshape(n * PEER_PICKS)
    hi, gate = _picks_token_major(hi), _picks_token_major(gate)
    w = _peer_act(idx, hi, h.reshape(n, SUBLANES, LANES), gate, tbl_u, tm_pass)
    if gt2.shape[1] == 1:
        gt3 = gt2.reshape(gt2.shape[0], 1, SUBLANES, LANES)
    else:
        gt3 = gt2.reshape(n // tm_pass, tm_pass, SUBLANES, LANES)
    out = _peer_out(idx, hi, w, x.reshape(n, SUBLANES, LANES), gt3, tbl_v, tm_pass)
    return out.reshape(n, D_MODEL)


def _rms(x, g):
    return x * lax.rsqrt(jnp.mean(x * x, axis=-1, keepdims=True) + EPS) * g


def _dot(a, b):
    return jnp.dot(a, b, preferred_element_type=F32)


def _dot_t(a, b):
    return lax.dot_general(a, b, (((1,), (1,)), ((), ())), preferred_element_type=F32)


def _sigmoid(x):
    return 1.0 / (1.0 + jnp.exp(-x))


def _softmax_update(m_ref, l_ref, acc_ref, s, v):
    m_old = m_ref[...]
    m_new = jnp.maximum(m_old, jnp.max(s, axis=1, keepdims=True))
    a = jnp.exp(m_old - m_new)
    p = jnp.exp(s - m_new)
    l_ref[...] = a * l_ref[...] + jnp.sum(p, axis=1, keepdims=True)
    acc_ref[...] = a * acc_ref[...] + _dot(p.astype(BF16), v)
    m_ref[...] = m_new


def _softmax_init(m_ref, l_ref, acc_ref):
    m_ref[...] = jnp.full(m_ref.shape, NEG_INF, F32)
    l_ref[...] = jnp.zeros(l_ref.shape, F32)
    acc_ref[...] = jnp.zeros(acc_ref.shape, F32)


def _topk_mask_lanes(imp, lane, k):
    sel = jnp.zeros(imp.shape, F32)
    for _ in range(k):
        m = jnp.max(imp, axis=1, keepdims=True)
        first = jnp.min(jnp.where(imp == m, lane, 1e9), axis=1, keepdims=True)
        hit = lane == first
        sel = jnp.where(hit, 1.0, sel)
        imp = jnp.where(hit, -jnp.inf, imp)
    return sel


def _ada_kernel(c_ref, w_ref, b_ref, o_ref):
    c = c_ref[...]
    o_ref[...] = jnp.dot(c * _sigmoid(c), w_ref[...], preferred_element_type=F32,
                         precision=lax.Precision.HIGHEST) + b_ref[...]


def _ada(c, w, b):
    nb, d = c.shape
    n_out = w.shape[1]
    tn = 1536
    return pl.pallas_call(
        _ada_kernel,
        grid=(n_out // tn,),
        in_specs=[pl.BlockSpec((nb, d), lambda j: (0, 0)),
                  pl.BlockSpec((d, tn), lambda j: (0, j)),
                  pl.BlockSpec((1, tn), lambda j: (0, j))],
        out_specs=pl.BlockSpec((nb, tn), lambda j: (0, j)),
        out_shape=jax.ShapeDtypeStruct((nb, n_out), F32),
        compiler_params=_cparams(("parallel",), 40),
        name="ada_mod",
    )(c, w, b)


_C_CQ, _C_CKV, _C_GA, _C_GB, _C_FQ, _C_FKV, _C_NQ, _C_NKV, _C_WKV, _C_END = (
    0, 256, 384, 512, 640, 1152, 1408, 1920, 2176, 2304)
_L_LOGF = ROPE_DIM
_L_GATE = ROPE_DIM + FOX_HEADS
_L_GEND = _L_GATE + 3 * NSA_HEADS
_Q_NOPE, _Q_ROPE, _Q_RSW, _Q_END = 0, 512, 1536, 2560


def _proj_kernel(x_ref, sc_ref, sh_ref, gm_ref, w1_ref, gq_ref, gkv_ref, wq_ref, wuk_ref, bf_ref, cm_ref, sm_ref,
                 mla_ref, fkv_ref, nkv_ref, win_ref, misc_ref,
                 qm_ref, kvm_ref, fq_ref, kvf_ref, nq_ref, kvn_ref, kvw_ref, cmp_ref):
    tm = x_ref.shape[0]
    h = _rms(x_ref[...], gm_ref[...]) * (1.0 + sc_ref[0]) + sh_ref[0]
    proj = _dot(h.astype(BF16), w1_ref[...])
    cm = cm_ref[...]
    sm = sm_ref[...]
    cqn = _rms(proj[:, _C_CQ:_C_CKV], gq_ref[...])
    q2 = _dot(cqn.astype(BF16), wq_ref[...])
    qlat = _dot(q2[:, _Q_NOPE:_Q_ROPE].astype(BF16), wuk_ref[...])
    for hd in range(MLA_HEADS):
        lo = hd * LANES
        rot = q2[:, _Q_ROPE + lo:_Q_ROPE + lo + LANES] * cm + q2[:, _Q_RSW + lo:_Q_RSW + lo + LANES] * sm
        qm_ref[:, 2 * lo:2 * lo + LANES] = (qlat[:, lo:lo + LANES] * MLA_SCALE).astype(BF16)
        qm_ref[:, 2 * lo + LANES:2 * lo + 2 * LANES] = (rot * MLA_SCALE).astype(BF16)
    ckvn = _rms(proj[:, _C_CKV:_C_GA], gkv_ref[...])
    ga = proj[:, _C_GA:_C_GB]
    krot = ga * cm + proj[:, _C_GB:_C_FQ] * sm
    mla_ref[:, 0:KV_RANK] = ckvn
    mla_ref[:, KV_RANK:KV_RANK + ROPE_DIM] = krot[:, 0:ROPE_DIM]
    kvm_ref[:, 0:LANES] = ckvn.astype(BF16)
    kvm_ref[:, LANES:2 * LANES] = krot.astype(BF16)
    lane = lax.broadcasted_iota(I32, (tm, LANES), 1)
    z = ga + bf_ref[...]
    logsig = jnp.minimum(z, 0.0) - jnp.log(1.0 + jnp.exp(-jnp.abs(z)))
    misc_ref[...] = jnp.where(lane < _L_LOGF, krot,
                              jnp.where(lane < _L_GATE, logsig,
                                        jnp.where(lane < _L_GEND, _sigmoid(ga), 0.0)))
    fq_ref[...] = (proj[:, _C_FQ:_C_FKV] * HD_SCALE).astype(BF16)
    nq_ref[...] = (proj[:, _C_NQ:_C_NKV] * HD_SCALE).astype(BF16)
    fkv = proj[:, _C_FKV:_C_NQ]
    nkv = proj[:, _C_NKV:_C_WKV]
    wkv = proj[:, _C_WKV:_C_END]
    fkv_ref[...] = fkv
    nkv_ref[...] = nkv
    win_ref[...] = wkv
    kvf_ref[...] = fkv.astype(BF16)
    kvn_ref[...] = nkv.astype(BF16)
    kvw_ref[...] = wkv.astype(BF16)
    cmp_ref[...] = jnp.sum(nkv[:, 0:LANES].reshape(tm // CMP_BLOCK, CMP_BLOCK, LANES), axis=1) * (1.0 / CMP_BLOCK)


def _proj(x, sc, sh, lw, cm, sm, tm):
    n = x.shape[0]
    nblk = n // tm
    bpg = nblk // sc.shape[0]
    pblk = cm.shape[0] // tm
    mod_spec = pl.BlockSpec((1,) + sc.shape[1:], lambda i: (i // bpg, 0, 0))
    const = lambda a: pl.BlockSpec(a.shape, lambda i: (0,) * a.ndim)
    tok = lambda w: pl.BlockSpec((tm, w), lambda i: (i, 0))
    rot_spec = pl.BlockSpec((tm, LANES), lambda i: (i % pblk, 0))
    widths = [(KV_RANK + ROPE_DIM, F32), (2 * LANES, F32), (2 * LANES, F32), (LANES, F32), (LANES, F32),
              (MLA_HEADS * 2 * LANES, BF16), (2 * LANES, BF16), (FOX_HEADS * LANES, BF16), (2 * LANES, BF16),
              (NSA_HEADS * LANES, BF16), (2 * LANES, BF16), (LANES, BF16)]
    out_specs = [tok(w) for w, _ in widths] + [pl.BlockSpec((tm // CMP_BLOCK, LANES), lambda i: (i, 0))]
    out_shape = [jax.ShapeDtypeStruct((n, w), dt) for w, dt in widths] + [
        jax.ShapeDtypeStruct((n // CMP_BLOCK, LANES), F32)]
    return pl.pallas_call(
        _proj_kernel,
        grid=(nblk,),
        in_specs=[tok(D_MODEL), mod_spec, mod_spec, const(lw["g_mix"]), const(lw["w1"]), const(lw["g_q"]),
                  const(lw["g_kv"]), const(lw["wq"]), const(lw["wuk"]), const(lw["bf"]), rot_spec, rot_spec],
        out_specs=out_specs,
        out_shape=out_shape,
        compiler_params=_cparams(("parallel",), 48),
        name="in_proj",
    )(x, sc, sh, lw["g_mix"], lw["w1"], lw["g_q"], lw["g_kv"], lw["wq"], lw["wuk"], lw["bf"], cm, sm)


def _split3(x):
    x1 = x.astype(BF16)
    r = x - x1.astype(F32)
    x2 = r.astype(BF16)
    x3 = (r - x2.astype(F32)).astype(BF16)
    return x1, x2, x3


def _cum_kernel(misc_ref, col_ref, row_ref):
    nblk = misc_ref.shape[1] // LANES
    ri = lax.broadcasted_iota(I32, (LANES, LANES), 0)
    ci = lax.broadcasted_iota(I32, (LANES, LANES), 1)
    tri = jnp.where(ci <= ri, 1.0, 0.0).astype(BF16)
    carry = jnp.zeros((1, LANES), F32)
    for blk in range(nblk):
        x1, x2, x3 = _split3(misc_ref[0, blk * LANES:(blk + 1) * LANES, :])
        c = (_dot(tri, x1) + _dot(tri, x2)) + _dot(tri, x3) + carry
        carry = c[LANES - 1:LANES, :]
        col_ref[0, blk * LANES:(blk + 1) * LANES, :] = c
        row_ref[0, blk] = c.T[_L_LOGF:_L_LOGF + SUBLANES, :]


def _cum(misc3):
    b, s, _ = misc3.shape
    return pl.pallas_call(
        _cum_kernel,
        grid=(b,),
        in_specs=[pl.BlockSpec((1, s, LANES), lambda i: (i, 0, 0))],
        out_specs=[pl.BlockSpec((1, s, LANES), lambda i: (i, 0, 0)),
                   pl.BlockSpec((1, s // LANES, SUBLANES, LANES), lambda i: (i, 0, 0, 0))],
        out_shape=[jax.ShapeDtypeStruct((b, s, LANES), F32),
                   jax.ShapeDtypeStruct((b, s // LANES, SUBLANES, LANES), F32)],
        compiler_params=_cparams(("parallel",)),
        name="fox_cumsum",
    )(misc3)


_T5_EXACT = N_BUCKETS // 2
_T5_THRESH = tuple(int(math.ceil(_T5_EXACT * (MAX_DISTANCE / _T5_EXACT) ** (j / (N_BUCKETS - _T5_EXACT)) - 1e-9))
                   for j in range(1, N_BUCKETS - _T5_EXACT))


def _t5_bias(tbl_ref, dist):
    n = jnp.maximum(dist, 0)
    big = jnp.full(n.shape, _T5_EXACT, I32)
    for t in _T5_THRESH:
        big = big + jnp.where(n >= t, 1, 0)
    bucket = jnp.where(n < _T5_EXACT, n, big)
    outs = [jnp.zeros(n.shape, F32) for _ in range(NSA_HEADS)]
    for j in range(N_BUCKETS):
        hit = bucket == j
        for h in range(NSA_HEADS):
            outs[h] = jnp.where(hit, tbl_ref[j, h], outs[h])
    return outs


def _cmp_block_of_lane(lane, half):
    return jnp.where(lane < half, 2 * lane, 2 * (lane - half) + 1)


def _t5_prompt_kernel(tbl_ref, toep_ref, cmpb_ref):
    nq = toep_ref.shape[1]
    ncmp = cmpb_ref.shape[3]
    i = lax.broadcasted_iota(I32, (QBLK, QBLK), 0)
    j = lax.broadcasted_iota(I32, (QBLK, QBLK), 1)
    ic = lax.broadcasted_iota(I32, (QBLK, ncmp), 0)
    lc = lax.broadcasted_iota(I32, (QBLK, ncmp), 1)
    cmp_end = _cmp_block_of_lane(lc, ncmp // 2) * CMP_BLOCK + (CMP_BLOCK - 1)
    for off in range(nq):
        for h, v in enumerate(_t5_bias(tbl_ref, off * QBLK + i - j)):
            toep_ref[h, off] = v
        for h, v in enumerate(_t5_bias(tbl_ref, off * QBLK + ic - cmp_end)):
            cmpb_ref[h, off] = v


def _t5_prompt(t5_table, s):
    nq = s // QBLK
    ncmp = s // CMP_BLOCK
    return pl.pallas_call(
        _t5_prompt_kernel,
        in_specs=[pl.BlockSpec(memory_space=pltpu.SMEM)],
        out_shape=[jax.ShapeDtypeStruct((NSA_HEADS, nq, QBLK, QBLK), F32),
                   jax.ShapeDtypeStruct((NSA_HEADS, nq, QBLK, ncmp), F32)],
        compiler_params=pltpu.CompilerParams(vmem_limit_bytes=40 * 1024 * 1024),
        name="t5_prompt_tables",
    )(t5_table)


def _t5_decode_kernel(tbl_ref, key_ref, near_ref, cmpb_ref, *, past_len, n_new):
    lp = key_ref.shape[2]
    ncmp = cmpb_ref.shape[2]
    wl = near_ref.shape[2]
    qi = lambda w: lax.broadcasted_iota(I32, (SUBLANES, w), 0)
    ln = lambda w: lax.broadcasted_iota(I32, (SUBLANES, w), 1)
    for h, v in enumerate(_t5_bias(tbl_ref, past_len + qi(lp) - ln(lp))):
        key_ref[h] = v
    for h, v in enumerate(_t5_bias(tbl_ref, WINDOW + qi(wl) - ln(wl))):
        near_ref[h] = v
    cmp_end = _cmp_block_of_lane(ln(ncmp), ncmp // 2) * CMP_BLOCK + (CMP_BLOCK - 1)
    for h, v in enumerate(_t5_bias(tbl_ref, past_len + qi(ncmp) - cmp_end)):
        cmpb_ref[h] = v


def _t5_decode(t5_table, past_len, n_new, w_buf):
    del w_buf
    return pl.pallas_call(
        functools.partial(_t5_decode_kernel, past_len=past_len, n_new=n_new),
        in_specs=[pl.BlockSpec(memory_space=pltpu.SMEM)],
        out_shape=[jax.ShapeDtypeStruct((NSA_HEADS, SUBLANES, past_len), F32),
                   jax.ShapeDtypeStruct((NSA_HEADS, SUBLANES, WINDOW + LANES), F32),
                   jax.ShapeDtypeStruct((NSA_HEADS, SUBLANES, past_len // CMP_BLOCK), F32)],
        compiler_params=pltpu.CompilerParams(vmem_limit_bytes=40 * 1024 * 1024),
        name="t5_decode_tables",
    )(t5_table)


def _attn_prompt_kernel(qm_ref, fq_ref, nq_ref, cumc_ref, misc_ref, cmpb_ref,
                        kvm_ref, kvf_ref, kvn_ref, kvw_ref, cmp_ref, cumr_ref, toep_ref, wuv_ref, eblk_ref,
                        mix_ref,
                        q8_scr, fq_scr, nq_scr, cq_scr, selk_scr, m_scr, l_scr, acc_scr):
    qi = pl.program_id(1)
    s_len = kvm_ref.shape[1]
    n_kb = s_len // QBLK
    n_cmp = s_len // CMP_BLOCK
    n_sel = s_len // SEL_BLOCK
    top = min(SEL_TOPK, n_sel)
    q0 = qi * QBLK

    def rows_pos(nh):
        r = lax.broadcasted_iota(I32, (nh * QBLK, QBLK), 0)
        return q0 + (r & (QBLK - 1))

    def key_pos(nh, kb):
        return kb * QBLK + lax.broadcasted_iota(I32, (nh * QBLK, QBLK), 1)

    for h in range(MLA_HEADS):
        q8_scr[h * QBLK:(h + 1) * QBLK, :] = qm_ref[:, h * 2 * LANES:(h + 1) * 2 * LANES]
    cumc = cumc_ref[0]
    for h in range(FOX_HEADS):
        fq_scr[h * QBLK:(h + 1) * QBLK, :] = fq_ref[:, h * LANES:(h + 1) * LANES]
        cq_scr[h * QBLK:(h + 1) * QBLK, :] = jnp.broadcast_to(cumc[:, _L_LOGF + h:_L_LOGF + h + 1], (QBLK, LANES))
    for h in range(NSA_HEADS):
        nq_scr[h * QBLK:(h + 1) * QBLK, :] = nq_ref[:, h * LANES:(h + 1) * LANES]

    nr = MLA_HEADS * QBLK
    m_r, l_r, a_r = m_scr, l_scr, acc_scr
    _softmax_init(m_r, l_r, a_r)
    qp8 = rows_pos(MLA_HEADS)

    def mla_step(kb, diagonal):
        k = kvm_ref[0, pl.ds(pl.multiple_of(kb * QBLK, QBLK), QBLK), :]
        s = _dot_t(q8_scr[...], k)
        if diagonal:
            s = jnp.where(key_pos(MLA_HEADS, kb) <= qp8, s, NEG_INF)
        _softmax_update(m_r, l_r, a_r, s, k[:, 0:LANES])

    def mla_body(kb, c):
        mla_step(kb, False)
        return c

    lax.fori_loop(0, qi, mla_body, 0)
    mla_step(qi, True)
    o_lat = (a_r[...] / l_r[...]).astype(BF16)
    o_mla = _dot(o_lat[0:QBLK], wuv_ref[0])
    for h in range(1, MLA_HEADS):
        o_mla = o_mla + _dot(o_lat[h * QBLK:(h + 1) * QBLK], wuv_ref[h])
    mix_ref[:, 0:MLA_HEADS * MLA_V_DIM] = o_mla.astype(BF16)

    nr = FOX_HEADS * QBLK
    m_r, l_r, a_r = m_scr.at[0:nr], l_scr.at[0:nr], acc_scr.at[0:nr]
    _softmax_init(m_r, l_r, a_r)
    qp4 = rows_pos(FOX_HEADS)

    def fox_step(kb, diagonal):
        k = kvf_ref[0, pl.ds(pl.multiple_of(kb * QBLK, QBLK), QBLK), :]
        ck = cumr_ref[0, kb]
        ck4 = jnp.concatenate([jnp.broadcast_to(ck[h:h + 1, :], (QBLK, LANES)) for h in range(FOX_HEADS)], axis=0)
        s = _dot_t(fq_scr[...], k[:, 0:LANES]) + (cq_scr[...] - ck4)
        if diagonal:
            s = jnp.where(key_pos(FOX_HEADS, kb) <= qp4, s, NEG_INF)
        _softmax_update(m_r, l_r, a_r, s, k[:, LANES:2 * LANES])

    def fox_body(kb, c):
        fox_step(kb, False)
        return c

    lax.fori_loop(0, qi, fox_body, 0)
    fox_step(qi, True)
    o_fox = a_r[...] / l_r[...]
    base = MLA_HEADS * MLA_V_DIM
    for h in range(FOX_HEADS):
        mix_ref[:, base + h * LANES:base + (h + 1) * LANES] = o_fox[h * QBLK:(h + 1) * QBLK].astype(BF16)

    half = n_cmp // 2
    cmpk = jnp.concatenate([cmp_ref[0, pl.ds(0, half, stride=2), :], cmp_ref[0, pl.ds(1, half, stride=2), :]],
                           axis=0).astype(BF16)
    nqv = nq_scr[...]
    bias_c = jnp.concatenate([cmpb_ref[h, 0] for h in range(NSA_HEADS)], axis=0)
    lane_c = lax.broadcasted_iota(I32, (nr, n_cmp), 1)
    cmp_end = _cmp_block_of_lane(lane_c, half) * CMP_BLOCK + (CMP_BLOCK - 1)
    qpc = q0 + (lax.broadcasted_iota(I32, (nr, n_cmp), 0) & (QBLK - 1))
    valid_c = cmp_end <= qpc
    s = jnp.where(valid_c, _dot_t(nqv, cmpk) + bias_c, NEG_INF)
    p = jnp.exp(s - jnp.max(s, axis=1, keepdims=True))
    pc = jnp.where(valid_c, p / jnp.sum(p, axis=1, keepdims=True), 0.0)
    o_cmp = _dot(pc.astype(BF16), cmpk)

    pcs = pc[0:QBLK]
    for h in range(1, NSA_HEADS):
        pcs = pcs + pc[h * QBLK:(h + 1) * QBLK]
    imp = pcs[:, 0:half] + pcs[:, half:n_cmp]
    imp_t = jnp.concatenate([imp, jnp.zeros((QBLK, LANES - n_sel), F32)], axis=1).T[0:n_sel, :]
    blk = lax.broadcasted_iota(I32, (n_sel, QBLK), 0)
    qps = q0 + lax.broadcasted_iota(I32, (n_sel, QBLK), 1)
    forced = (blk == qps // SEL_BLOCK) | (blk == 0)
    imp_t = jnp.where(forced, FORCE_SCORE, imp_t)
    imp_t = jnp.where(blk * SEL_BLOCK > qps, -1.0, imp_t)
    rank = jnp.zeros((n_sel, QBLK), F32)
    for i in range(n_sel):
        row = imp_t[i:i + 1, :]
        rank = rank + jnp.where((row > imp_t) | ((row == imp_t) & (blk > i)), 1.0, 0.0)
    sel_t = jnp.where(rank < top, 1.0, 0.0)
    sel = jnp.concatenate([sel_t, jnp.zeros((LANES - n_sel, QBLK), F32)], axis=0).T
    selk = _dot(sel.astype(BF16), eblk_ref[...])
    for kb in range(n_kb):
        selk_scr[kb] = selk[:, kb * QBLK:(kb + 1) * QBLK]

    def toep4(off):
        return jnp.concatenate([toep_ref[h, off] for h in range(NSA_HEADS)], axis=0)

    _softmax_init(m_r, l_r, a_r)

    def sel_step(kb, c):
        k = kvn_ref[0, pl.ds(pl.multiple_of(kb * QBLK, QBLK), QBLK), :][:, LANES:2 * LANES]
        s = _dot_t(nq_scr[...], k) + toep4(qi - kb)
        mk = selk_scr[kb]
        mk4 = jnp.concatenate([mk] * NSA_HEADS, axis=0)
        s = jnp.where((key_pos(NSA_HEADS, kb) <= qp4) & (mk4 > 0.5), s, NEG_INF)
        _softmax_update(m_r, l_r, a_r, s, k)
        return c

    lax.fori_loop(0, qi + 1, sel_step, 0)
    o_sel = a_r[...] / l_r[...]

    _softmax_init(m_r, l_r, a_r)

    def win_step(kb, c):
        k = kvw_ref[0, pl.ds(pl.multiple_of(kb * QBLK, QBLK), QBLK), :]
        s = _dot_t(nq_scr[...], k) + toep4(qi - kb)
        dist = qp4 - key_pos(NSA_HEADS, kb)
        s = jnp.where((dist >= 0) & (dist <= WINDOW), s, NEG_INF)
        _softmax_update(m_r, l_r, a_r, s, k)
        return c

    lax.fori_loop(jnp.maximum(qi - WINDOW // QBLK, 0), qi + 1, win_step, 0)
    o_win = a_r[...] / l_r[...]

    misc = misc_ref[...]
    base = MLA_HEADS * MLA_V_DIM + FOX_HEADS * LANES
    for h in range(NSA_HEADS):
        rows = slice(h * QBLK, (h + 1) * QBLK)
        gate = lambda c: misc[:, _L_GATE + c * NSA_HEADS + h:_L_GATE + c * NSA_HEADS + h + 1]
        o = gate(0) * o_cmp[rows] + gate(1) * o_sel[rows] + gate(2) * o_win[rows]
        mix_ref[:, base + h * LANES:base + (h + 1) * LANES] = o.astype(BF16)


_MIX_PAD = MLA_HEADS * MLA_V_DIM + (FOX_HEADS + NSA_HEADS) * LANES


def _attn_prompt(po, cumc, cumr, toep, cmpb, wuv, b, s):
    nq = s // QBLK
    qblk = lambda w: pl.BlockSpec((QBLK, w), lambda bi, qi: (bi * nq + qi, 0))
    per_b = lambda a: pl.BlockSpec((1,) + a.shape[1:], lambda bi, qi: (bi,) + (0,) * (a.ndim - 1))
    const = lambda a: pl.BlockSpec(a.shape, lambda bi, qi: (0,) * a.ndim)
    kvm = po["kvm"].reshape(b, s, 2 * LANES)
    kvf = po["kvf"].reshape(b, s, 2 * LANES)
    kvn = po["kvn"].reshape(b, s, 2 * LANES)
    kvw = po["kvw"].reshape(b, s, LANES)
    cmpm = po["cmp"].reshape(b, s // CMP_BLOCK, LANES)
    rows = MLA_HEADS * QBLK
    assert s // SEL_BLOCK <= LANES
    eblk = (jnp.arange(s, dtype=I32)[None, :] // SEL_BLOCK == jnp.arange(LANES, dtype=I32)[:, None]).astype(BF16)
    return pl.pallas_call(
        _attn_prompt_kernel,
        grid=(b, nq),
        in_specs=[qblk(MLA_HEADS * 2 * LANES), qblk(FOX_HEADS * LANES), qblk(NSA_HEADS * LANES),
                  pl.BlockSpec((1, QBLK, LANES), lambda bi, qi: (bi, qi, 0)), qblk(LANES),
                  pl.BlockSpec((NSA_HEADS, 1, QBLK, s // CMP_BLOCK), lambda bi, qi: (0, qi, 0, 0)),
                  per_b(kvm), per_b(kvf), per_b(kvn), per_b(kvw), per_b(cmpm), per_b(cumr),
                  const(toep), const(wuv), const(eblk)],
        out_specs=qblk(_MIX_PAD),
        out_shape=jax.ShapeDtypeStruct((b * s, _MIX_PAD), BF16),
        scratch_shapes=[pltpu.VMEM((rows, 2 * LANES), BF16),
                        pltpu.VMEM((FOX_HEADS * QBLK, LANES), BF16),
                        pltpu.VMEM((NSA_HEADS * QBLK, LANES), BF16),
                        pltpu.VMEM((FOX_HEADS * QBLK, LANES), F32),
                        pltpu.VMEM((nq, QBLK, QBLK), F32),
                        pltpu.VMEM((rows, LANES), F32), pltpu.VMEM((rows, LANES), F32),
                        pltpu.VMEM((rows, LANES), F32)],
        compiler_params=_cparams(("parallel", "arbitrary"), 48),
        name="attn_prompt",
    )(po["qm"], po["fq"], po["nq"], cumc, po["misc"], cmpb, kvm, kvf, kvn, kvw, cmpm, cumr, toep, wuv, eblk)


def _wo_kernel(mix_ref, x_ref, gt_ref, w_ref, o_ref):
    o_ref[...] = x_ref[...] + gt_ref[0] * _dot(mix_ref[...], w_ref[...])


def _wo(mix, x, gt, w, tm):
    n = x.shape[0]
    nblk = n // tm
    bpg = nblk // gt.shape[0]
    return pl.pallas_call(
        _wo_kernel,
        grid=(nblk,),
        in_specs=[pl.BlockSpec((tm, mix.shape[1]), lambda i: (i, 0)),
                  pl.BlockSpec((tm, D_MODEL), lambda i: (i, 0)),
                  pl.BlockSpec((1,) + gt.shape[1:], lambda i: (i // bpg, 0, 0)),
                  pl.BlockSpec(w.shape, lambda i: (0, 0))],
        out_specs=pl.BlockSpec((tm, D_MODEL), lambda i: (i, 0)),
        out_shape=jax.ShapeDtypeStruct((n, D_MODEL), F32),
        compiler_params=_cparams(("parallel",), 40),
        name="out_proj",
    )(mix, x, gt, w)


def _final_norm_kernel(x_ref, g_ref, o_ref):
    o_ref[...] = _rms(x_ref[...], g_ref[...])


def _final_norm(x, g, tm):
    n = x.shape[0]
    return pl.pallas_call(
        _final_norm_kernel,
        grid=(n // tm,),
        in_specs=[pl.BlockSpec((tm, D_MODEL), lambda i: (i, 0)), pl.BlockSpec((1, D_MODEL), lambda i: (0, 0))],
        out_specs=pl.BlockSpec((tm, D_MODEL), lambda i: (i, 0)),
        out_shape=jax.ShapeDtypeStruct((n, D_MODEL), F32),
        compiler_params=_cparams(("parallel",)),
        name="final_norm",
    )(x, g)


_NEW_PAD = LANES


def _rows_from_tokens(tok, n_rows, per):
    r = lax.broadcasted_iota(I32, (n_rows, LANES), 0)
    out = jnp.zeros((n_rows, LANES), F32)
    for i in range(n_rows // per):
        out = jnp.where(r // per == i, jnp.broadcast_to(tok[i:i + 1, :], (n_rows, LANES)), out)
    return out


def _rep_heads(x, row0, n_rows):
    r = lax.broadcasted_iota(I32, (n_rows, LANES), 0) & (FOX_HEADS - 1)
    out = jnp.zeros((n_rows, LANES), F32)
    for h in range(FOX_HEADS):
        out = jnp.where(r == h, jnp.broadcast_to(x[row0 + h:row0 + h + 1, :], (n_rows, LANES)), out)
    return out


def _softmax_update_wide(m_ref, l_ref, acc_ref, s, vt):
    m_old = m_ref[...]
    m_new = jnp.maximum(m_old, jnp.max(s, axis=1, keepdims=True))
    a = jnp.exp(m_old - m_new)
    p = jnp.exp(s - m_new[:, 0:1])
    l_ref[...] = a * l_ref[...] + jnp.sum(p, axis=1, keepdims=True)
    acc_ref[...] = a * acc_ref[...] + _dot_t(p.astype(BF16), vt)
    m_ref[...] = m_new


def _pick_lane(x, lane_of_row):
    lane = lax.broadcasted_iota(I32, x.shape, 1)
    return jnp.sum(jnp.where(lane == lane_of_row, x, 0.0), axis=1, keepdims=True)


def _decode1_kernel(pt_ref, qm_ref, fq_ref, newm_ref, newf_ref, cnew_ref, wuv_ref, *rest, pp, n_new):
    del pt_ref
    pages = rest[:4 * pp]
    omla_ref, ofox_ref, cmpm_ref = rest[4 * pp:4 * pp + 3]
    m1, l1, a1, m2, l2, a2, suf, newc = rest[4 * pp + 3:]
    jj = pl.program_id(1)
    r_m = MLA_HEADS * n_new
    r_f = FOX_HEADS * n_new
    row_f = lax.broadcasted_iota(I32, (r_f, LANES), 0)
    lane_f = lax.broadcasted_iota(I32, (r_f, LANES), 1)
    head_lane = _L_LOGF + (row_f & (FOX_HEADS - 1))

    @pl.when(jj == 0)
    def _():
        _softmax_init(m1, l1, a1)
        _softmax_init(m2, l2, a2)
        suf[...] = jnp.zeros(suf.shape, F32)
        x = cnew_ref[0]
        sub = lax.broadcasted_iota(I32, (SUBLANES, LANES), 0)
        y = x + jnp.where(sub >= 1, pltpu.roll(x, 1, 0), 0.0)
        y = y + jnp.where(sub >= 2, pltpu.roll(y, 2, 0), 0.0)
        newc[0:SUBLANES, :] = y
        col = _pick_lane(_rows_from_tokens(y, r_f, FOX_HEADS), head_lane)
        newc[SUBLANES:SUBLANES + r_f, :] = jnp.broadcast_to(col, (r_f, LANES))

    q = qm_ref[0]
    fq = fq_ref[0]
    ri = lax.broadcasted_iota(I32, (LANES, LANES), 0)
    ci_ = lax.broadcasted_iota(I32, (LANES, LANES), 1)
    later = jnp.where(ri > ci_, 1.0, 0.0).astype(BF16)
    blocks_per_page = PAGE_SIZE // CMP_BLOCK
    ra = lax.broadcasted_iota(I32, (2 * SUBLANES, LANES), 0)
    la = lax.broadcasted_iota(I32, (2 * SUBLANES, LANES), 1)
    in_block = jnp.where((la // CMP_BLOCK == ra) & (ra < blocks_per_page), 1.0, 0.0).astype(BF16)
    c_t, kr_t, kf_t, vf_t, lfs = [], [], [], [], []
    for i in range(pp):
        pm, pf, plf, pn = pages[4 * i:4 * i + 4]
        km = pm[0, 0]
        c_t.append(km[0:KV_RANK].astype(BF16))
        kr_t.append(km[KV_RANK:KV_RANK + ROPE_DIM].astype(BF16))
        kf = pf[0, 0]
        kf_t.append(kf[0:LANES].astype(BF16))
        vf_t.append(kf[LANES:2 * LANES].astype(BF16))
        lfs.append(plf[0, 0])
        kc = pn[0, 0]
        k1 = kc.astype(BF16)
        k2 = (kc - k1.astype(F32)).astype(BF16)
        means = (_dot_t(in_block, k1) + _dot_t(in_block, k2))[0:blocks_per_page] * (1.0 / CMP_BLOCK)
        cmpm_ref[0, pp - 1 - i] = means
    c_all = jnp.concatenate(c_t, axis=1)
    s = _dot(q[:, 0:KV_RANK], c_all) + _dot(q[:, KV_RANK:KV_RANK + ROPE_DIM], jnp.concatenate(kr_t, axis=1))
    _softmax_update_wide(m1, l1, a1, s, c_all)
    lf_all = jnp.concatenate(lfs, axis=0)
    n_lf = lf_all.shape[0]
    if n_lf % (2 * SUBLANES):
        lf_all = jnp.concatenate([lf_all, jnp.zeros((2 * SUBLANES - n_lf % (2 * SUBLANES), LANES), F32)], axis=0)
    x1, x2, x3 = _split3(lf_all)
    exc_all = (_dot(x1, later) + _dot(x2, later)) + _dot(x3, later)
    tot_all = jnp.broadcast_to(exc_all[:, 0:1] + lf_all[:, 0:1], exc_all.shape)
    after = suf[...]
    new_part = newc[SUBLANES:SUBLANES + r_f, :]
    decays = []
    for i in range(pp):
        decays.append(_rep_heads(exc_all, i * FOX_HEADS, r_f) + (after + new_part))
        after = after + _rep_heads(tot_all, i * FOX_HEADS, r_f)
    suf[...] = after
    s = _dot(fq, jnp.concatenate(kf_t, axis=1)) + jnp.concatenate(decays, axis=1)
    _softmax_update_wide(m2, l2, a2, s, jnp.concatenate(vf_t, axis=1))

    @pl.when(jj == pl.num_programs(1) - 1)
    def _():
        lane_m = lax.broadcasted_iota(I32, (r_m, _NEW_PAD), 1)
        row_m = lax.broadcasted_iota(I32, (r_m, _NEW_PAD), 0)
        kn = newm_ref[0]
        s = jnp.where((lane_m <= row_m // MLA_HEADS) & (lane_m < n_new), _dot_t(q, kn), NEG_INF)
        _softmax_update(m1, l1, a1, s, kn[:, 0:KV_RANK])
        o_lat = (a1[...] / l1[...]).astype(BF16)
        rr = lax.broadcasted_iota(I32, (r_m, LANES), 0)
        o = jnp.zeros((r_m, MLA_HEADS * MLA_V_DIM), F32)
        for h in range(MLA_HEADS):
            o = o + _dot(jnp.where((rr & (MLA_HEADS - 1)) == h, o_lat, jnp.zeros_like(o_lat)), wuv_ref[h])
        omla_ref[0] = jnp.sum(o.reshape(n_new, MLA_HEADS, MLA_HEADS * MLA_V_DIM), axis=1)

        kf = newf_ref[0]
        yc = newc[0:SUBLANES, :]
        ci = newc[SUBLANES:SUBLANES + r_f, :]
        lane_n = lax.broadcasted_iota(I32, (r_f, _NEW_PAD), 1)
        row_n = lax.broadcasted_iota(I32, (r_f, _NEW_PAD), 0)
        decay = jnp.zeros((r_f, _NEW_PAD), F32)
        for j in range(n_new):
            cj = _pick_lane(jnp.broadcast_to(yc[j:j + 1, :], (r_f, LANES)), head_lane)
            decay = jnp.where(lane_n == j, ci[:, 0:_NEW_PAD] - cj, decay)
        s = _dot_t(fq, kf[:, 0:LANES]) + decay
        s = jnp.where((lane_n <= row_n // FOX_HEADS) & (lane_n < n_new), s, NEG_INF)
        _softmax_update(m2, l2, a2, s, kf[:, LANES:2 * LANES])
        ofox_ref[0] = a2[...] / l2[...]


def _decode1(layer, page_table, qm, fq, newm, newf, cnew, wuv, cache_mla, cache_fox, logf_t, cache_nsa, pp):
    b, n_pages = page_table.shape
    n_new = qm.shape[1] // MLA_HEADS
    steps = n_pages // pp
    per_b = lambda a: pl.BlockSpec((1,) + a.shape[1:], lambda bi, jj, pt: (bi,) + (0,) * (a.ndim - 1))

    def page_spec(feats, i):
        def imap(bi, jj, pt):
            return (layer, pt[bi, n_pages - 1 - (jj * pp + i)], 0, 0)
        return pl.BlockSpec((1, 1, feats, PAGE_SIZE), imap)

    in_specs = [per_b(qm), per_b(fq), per_b(newm), per_b(newf), per_b(cnew),
                pl.BlockSpec(wuv.shape, lambda bi, jj, pt: (0, 0, 0))]
    args = [qm, fq, newm, newf, cnew, wuv]
    for i in range(pp):
        in_specs += [page_spec(KV_RANK + ROPE_DIM, i), page_spec(2 * LANES, i),
                     page_spec(FOX_HEADS, i), page_spec(LANES, i)]
        args += [cache_mla, cache_fox, logf_t, cache_nsa]
    r_m, r_f = MLA_HEADS * n_new, FOX_HEADS * n_new
    grid_spec = pltpu.PrefetchScalarGridSpec(
        num_scalar_prefetch=1, grid=(b, steps), in_specs=in_specs,
        out_specs=[pl.BlockSpec((1, n_new, MLA_HEADS * MLA_V_DIM), lambda bi, jj, pt: (bi, 0, 0)),
                   pl.BlockSpec((1, r_f, LANES), lambda bi, jj, pt: (bi, 0, 0)),
                   pl.BlockSpec((1, pp, PAGE_SIZE // CMP_BLOCK, LANES), lambda bi, jj, pt: (bi, steps - 1 - jj, 0, 0))],
        scratch_shapes=[pltpu.VMEM((r_m, LANES), F32), pltpu.VMEM((r_m, LANES), F32), pltpu.VMEM((r_m, LANES), F32),
                        pltpu.VMEM((r_f, LANES), F32), pltpu.VMEM((r_f, LANES), F32), pltpu.VMEM((r_f, LANES), F32),
                        pltpu.VMEM((r_f, LANES), F32), pltpu.VMEM((SUBLANES + r_f, LANES), F32)])
    return pl.pallas_call(
        functools.partial(_decode1_kernel, pp=pp, n_new=n_new),
        grid_spec=grid_spec,
        out_shape=[jax.ShapeDtypeStruct((b, n_new, MLA_HEADS * MLA_V_DIM), F32),
                   jax.ShapeDtypeStruct((b, r_f, LANES), F32),
                   jax.ShapeDtypeStruct((b, n_pages, PAGE_SIZE // CMP_BLOCK, LANES), F32)],
        compiler_params=_cparams(("parallel", "arbitrary"), 40),
        name="decode_mla_fox",
    )(page_table, *args)


def _decode2_kernel(pt_ref, nq_ref, misc_ref, cmp_ref, bkey_ref, bnear_ref, bcmp_ref, win_ref, neww_ref, newn_ref,
                    eblk_ref, *rest, pp, n_new):
    del pt_ref
    pages = rest[:pp]
    onsa_ref = rest[pp]
    selk, ocmp, m, l, acc = rest[pp + 1:]
    jj = pl.program_id(1)
    r_n = NSA_HEADS * n_new
    nq = nq_ref[0]
    row = lax.broadcasted_iota(I32, (r_n, LANES), 0)
    lane = lax.broadcasted_iota(I32, (r_n, LANES), 1)

    @pl.when(jj == 0)
    def _():
        cmpk = cmp_ref[0].astype(BF16)
        s = _dot_t(nq, cmpk) + bcmp_ref[...]
        p = jnp.exp(s - jnp.max(s, axis=1, keepdims=True))
        pc = p / jnp.sum(p, axis=1, keepdims=True)
        ocmp[...] = _dot(pc.astype(BF16), cmpk)
        gr = lax.broadcasted_iota(I32, (r_n, r_n), 0) // NSA_HEADS
        gc = lax.broadcasted_iota(I32, (r_n, r_n), 1) // NSA_HEADS
        same_q = jnp.where(gr == gc, 1.0, 0.0).astype(BF16)
        p1, p2, p3 = _split3(pc)
        pcs = (_dot(same_q, p1) + _dot(same_q, p2)) + _dot(same_q, p3)
        half = pcs.shape[1] // 2
        imp = pcs[:, 0:half] + pcs[:, half:2 * half]
        lane_b = lax.broadcasted_iota(I32, imp.shape, 1)
        imp = jnp.where(lane_b == 0, FORCE_SCORE, imp)
        n_sel = half + 1
        sel = _topk_mask_lanes(imp, lane_b.astype(F32), min(SEL_TOPK, n_sel) - 1)
        if half < LANES:
            sel = jnp.concatenate([sel, jnp.zeros((r_n, LANES - half), F32)], axis=1)
        key_sel = _dot(sel.astype(BF16), eblk_ref[...])
        step_w = pp * LANES
        for st in range(selk.shape[0]):
            selk[st] = key_sel[:, st * step_w:(st + 1) * step_w]
        _softmax_init(m, l, acc)

    k_all = jnp.concatenate([pages[i][0, 0].astype(BF16) for i in range(pp)], axis=1)
    s = _dot(nq, k_all) + bkey_ref[...]
    s = jnp.where(selk[jj] > 0.5, s, NEG_INF)
    _softmax_update_wide(m, l, acc, s, k_all)

    @pl.when(jj == pl.num_programs(1) - 1)
    def _():
        lane_n = lax.broadcasted_iota(I32, (r_n, _NEW_PAD), 1)
        row_n = lax.broadcasted_iota(I32, (r_n, _NEW_PAD), 0)
        own = (lane_n <= row_n // NSA_HEADS) & (lane_n < n_new)
        bnew = bnear_ref[:, WINDOW:WINDOW + LANES][:, 0:_NEW_PAD]
        kn = newn_ref[0]
        s = jnp.where(own, _dot_t(nq, kn) + bnew, NEG_INF)
        _softmax_update(m, l, acc, s, kn)
        o_sel = acc[...] / l[...]
        _softmax_init(m, l, acc)
        for c in range(WINDOW // LANES):
            kw = win_ref[0, c * LANES:(c + 1) * LANES, :].astype(BF16)
            s = _dot_t(nq, kw) + bnear_ref[:, c * LANES:(c + 1) * LANES]
            if c == 0:
                s = jnp.where(lane >= row // NSA_HEADS, s, NEG_INF)
            _softmax_update(m, l, acc, s, kw)
        kwn = neww_ref[0]
        s = jnp.where(own, _dot_t(nq, kwn) + bnew, NEG_INF)
        _softmax_update(m, l, acc, s, kwn)
        o_win = acc[...] / l[...]
        g16 = _rows_from_tokens(misc_ref[0], r_n, NSA_HEADS)
        gate = lambda c: _pick_lane(g16, _L_GATE + c * NSA_HEADS + (row & (NSA_HEADS - 1)))
        onsa_ref[0] = gate(0) * ocmp[...] + gate(1) * o_sel + gate(2) * o_win


def _decode2(layer, page_table, nq, misc, cmp_eo, bkey, bnear, bcmp, win, neww, newn, cache_nsa, pp):
    b, n_pages = page_table.shape
    r_n = nq.shape[1]
    n_new = r_n // NSA_HEADS
    steps = n_pages // pp
    per_b = lambda a: pl.BlockSpec((1,) + a.shape[1:], lambda bi, jj, pt: (bi,) + (0,) * (a.ndim - 1))
    const = lambda a: pl.BlockSpec(a.shape, lambda bi, jj, pt: (0,) * a.ndim)
    assert n_pages * PAGE_SIZE // SEL_BLOCK <= LANES
    eblk = (jnp.arange(n_pages * PAGE_SIZE, dtype=I32)[None, :] // SEL_BLOCK
            == jnp.arange(LANES, dtype=I32)[:, None]).astype(BF16)
    in_specs = [per_b(nq), per_b(misc), per_b(cmp_eo),
                pl.BlockSpec((r_n, pp * LANES), lambda bi, jj, pt: (0, jj)), const(bnear), const(bcmp),
                per_b(win), per_b(neww), per_b(newn), const(eblk)]
    args = [nq, misc, cmp_eo, bkey, bnear, bcmp, win, neww, newn, eblk]
    for i in range(pp):
        in_specs.append(pl.BlockSpec((1, 1, LANES, PAGE_SIZE),
                                     lambda bi, jj, pt, i=i: (layer, pt[bi, jj * pp + i], 1, 0)))
        args.append(cache_nsa)
    grid_spec = pltpu.PrefetchScalarGridSpec(
        num_scalar_prefetch=1, grid=(b, steps), in_specs=in_specs,
        out_specs=pl.BlockSpec((1, r_n, LANES), lambda bi, jj, pt: (bi, 0, 0)),
        scratch_shapes=[pltpu.VMEM((steps, r_n, pp * LANES), F32), pltpu.VMEM((r_n, LANES), F32),
                        pltpu.VMEM((r_n, LANES), F32), pltpu.VMEM((r_n, LANES), F32), pltpu.VMEM((r_n, LANES), F32)])
    return pl.pallas_call(
        functools.partial(_decode2_kernel, pp=pp, n_new=n_new),
        grid_spec=grid_spec,
        out_shape=jax.ShapeDtypeStruct((b, r_n, LANES), F32),
        compiler_params=_cparams(("parallel", "arbitrary"), 40),
        name="decode_nsa",
    )(page_table, *args)


def _rope_swap(w):
    half = ROPE_DIM // 2
    return jnp.concatenate([w[..., half:], w[..., :half]], axis=-1)


def _pad_lanes(w, width=LANES, at=0):
    out = jnp.zeros(w.shape[:-1] + (width,), w.dtype)
    return out.at[..., at:at + w.shape[-1]].set(w)


def _prep_layer(l, w_in, b_f, g_q, g_kv, g_mix, w_uq, w_uk, w_uv, w_o):
    offs = np.concatenate([[0], np.cumsum(IN_SPLITS)])
    col = lambda i: w_in[l][:, offs[i]:offs[i + 1]]
    cq, ckv, kr, fq, fk, fv, ff, nq, kc, vc, ks, vs, kw, vw, ng = [col(i) for i in range(len(IN_SPLITS))]
    ga = _pad_lanes(jnp.concatenate([kr, ff, ng], axis=1))
    gb = _pad_lanes(_rope_swap(kr))
    fq4 = [_pad_lanes(fq[:, h * HEAD_DIM:(h + 1) * HEAD_DIM], at=(h // FOX_GROUP) * HEAD_DIM) for h in range(FOX_HEADS)]
    nq4 = [_pad_lanes(nq[:, h * HEAD_DIM:(h + 1) * HEAD_DIM]) for h in range(NSA_HEADS)]
    w1 = jnp.concatenate([cq, ckv, ga, gb] + fq4 + [fk, fv] + nq4 + [kc, vc, ks, vs, kw, vw], axis=1).astype(BF16)
    uq = w_uq[l]
    nope = uq[:, :, :NOPE_DIM].reshape(Q_RANK, MLA_HEADS * NOPE_DIM)
    rope = _pad_lanes(uq[:, :, NOPE_DIM:]).reshape(Q_RANK, MLA_HEADS * LANES)
    rsw = _pad_lanes(_rope_swap(uq[:, :, NOPE_DIM:])).reshape(Q_RANK, MLA_HEADS * LANES)
    wq = jnp.concatenate([nope, rope, rsw], axis=1).astype(BF16)
    wuk = jnp.zeros((MLA_HEADS * NOPE_DIM, MLA_HEADS * KV_RANK), F32)
    wuv = jnp.zeros((MLA_HEADS, KV_RANK, MLA_HEADS * MLA_V_DIM), F32)
    for h in range(MLA_HEADS):
        wuk = wuk.at[h * NOPE_DIM:(h + 1) * NOPE_DIM, h * KV_RANK:(h + 1) * KV_RANK].set(w_uk[l][:, h, :].T)
        wuv = wuv.at[h, :, h * MLA_V_DIM:(h + 1) * MLA_V_DIM].set(w_uv[l][:, h, :])
    wo = jnp.zeros((_MIX_PAD, D_MODEL), F32)
    n_mla = MLA_HEADS * MLA_V_DIM
    wo = wo.at[0:n_mla].set(w_o[l][0:n_mla])
    for h in range(FOX_HEADS):
        r0 = n_mla + h * LANES + (h // FOX_GROUP) * HEAD_DIM
        wo = wo.at[r0:r0 + HEAD_DIM].set(w_o[l][n_mla + h * HEAD_DIM:n_mla + (h + 1) * HEAD_DIM])
    for h in range(NSA_HEADS):
        r0 = n_mla + (FOX_HEADS + h) * LANES + HEAD_DIM
        src = n_mla + (FOX_HEADS + h) * HEAD_DIM
        wo = wo.at[r0:r0 + HEAD_DIM].set(w_o[l][src:src + HEAD_DIM])
    return dict(w1=w1, wq=wq, wuk=wuk.astype(BF16), wuv=wuv.astype(BF16), wo=wo.astype(BF16),
                g_mix=g_mix[l][None], g_q=g_q[l][None], g_kv=g_kv[l][None],
                bf=_pad_lanes(b_f[l][None], at=_L_LOGF))


def _rope_tables(pos):
    half = ROPE_DIM // 2
    inv = ROPE_BASE ** (-jnp.arange(half, dtype=F32) / half)
    ang = pos.astype(F32)[:, None] * inv
    cos, sin = jnp.cos(ang), jnp.sin(ang)
    return _pad_lanes(jnp.concatenate([cos, cos], axis=1)), _pad_lanes(jnp.concatenate([-sin, sin], axis=1))


_PROJ_NAMES = ("mla", "fkv", "nkv", "win", "misc", "qm", "kvm", "fq", "kvf", "nq", "kvn", "kvw", "cmp")


def _pad_rows(a, rows):
    return jnp.pad(a, ((0, 0), (0, rows - a.shape[1]), (0, 0)))


def kernel(x_prompt, x_sample, c_prompt, c_sample, cache_mla, cache_fox_kv, cache_fox_logf, cache_nsa_kv, state_nsa_win, page_table, w_ada, b_ada, g_mix, g_ffn, g_final, w_in, b_f, g_q, g_kv, w_uq, w_uk, w_uv, w_o, t5_table, w_pq, sub_keys, expert_u, expert_v):
    bp, sp, d = x_prompt.shape
    bs, ss, _ = x_sample.shape
    depth = w_in.shape[0]
    n_pool = cache_mla.shape[1]
    n_pages = page_table.shape[1]
    past_len = n_pages * PAGE_SIZE
    assert state_nsa_win.shape[2] == WINDOW and sp % QBLK == 0 and sp > QBLK
    np_, ns_ = bp * sp, bs * ss
    tm_p = 256
    tm_s = min(256, ns_)
    pp = next(p for p in (8, 4, 2, 1) if n_pages % p == 0)

    xp = x_prompt.reshape(np_, d)
    xs = x_sample.reshape(ns_, d)
    cache_mla = cache_mla.swapaxes(2, 3)
    cache_fox = cache_fox_kv.reshape(depth, n_pool, PAGE_SIZE, 2 * FOX_KV_HEADS * HEAD_DIM).swapaxes(2, 3)
    cache_nsa = cache_nsa_kv.reshape(depth, n_pool, PAGE_SIZE, 4 * HEAD_DIM).swapaxes(2, 3)
    logf_t = cache_fox_logf.swapaxes(2, 3)
    win_buf = state_nsa_win.reshape(depth, bs, WINDOW, 2 * HEAD_DIM)

    cm_p, sm_p = _rope_tables(jnp.arange(sp, dtype=I32))
    cm_s, sm_s = _rope_tables(past_len + jnp.arange(ss, dtype=I32))
    cm_s, sm_s = jnp.tile(cm_s, (tm_s // ss, 1)), jnp.tile(sm_s, (tm_s // ss, 1))
    toep, cmpb = _t5_prompt(t5_table, sp)
    bkey, bnear, bcmp = _t5_decode(t5_table, past_len, ss, WINDOW)
    rows_ih = lambda t: t[:, :ss].transpose(1, 0, 2).reshape(ss * NSA_HEADS, t.shape[2])
    bkey, bnear, bcmp = rows_ih(bkey), rows_ih(bnear), rows_ih(bcmp)

    c_all = jnp.concatenate([c_prompt, c_sample], axis=0)
    st_p, st_s = [], []
    for l in range(depth):
        lw = _prep_layer(l, w_in, b_f, g_q, g_kv, g_mix, w_uq, w_uk, w_uv, w_o)
        mod = _ada(c_all, w_ada[l], b_ada[l][None])
        mod_p = [m[:, None, :] for m in jnp.split(mod[:bp], 6, axis=-1)]
        mod_s = [jnp.repeat(m, ss, axis=0).reshape(ns_ // tm_s, tm_s, d) for m in jnp.split(mod[bp:], 6, axis=-1)]
        wpq = w_pq[l].astype(BF16)
        keys = sub_keys[l].reshape(PEER_HEADS * 2, N_KEYS, PEER_KEY_DIM).astype(BF16)
        tbl_u, tbl_v = _pack_table(expert_u[l]), _pack_table(expert_v[l])

        po = dict(zip(_PROJ_NAMES, _proj(xp, mod_p[1], mod_p[0], lw, cm_p, sm_p, tm_p)))
        cumc, cumr = _cum(po["misc"].reshape(bp, sp, LANES))
        mix = _attn_prompt(po, cumc, cumr, toep, cmpb, lw["wuv"], bp, sp)
        xp = _wo(mix, xp, mod_p[2], lw["wo"], tm_p)
        xp = _peer_block(xp, mod_p[4], mod_p[3], mod_p[5], g_ffn[l][None], wpq, keys, tbl_u, tbl_v, tm_p, 128)
        w_keep = min(WINDOW, sp)
        st_p.append((po["mla"].reshape(bp, sp, -1),
                     po["fkv"].reshape(bp, sp, 2, FOX_KV_HEADS, HEAD_DIM),
                     po["misc"][:, _L_LOGF:_L_GATE].reshape(bp, sp, FOX_HEADS),
                     po["nkv"].reshape(bp, sp, 4, HEAD_DIM),
                     po["win"].reshape(bp, sp, 2, HEAD_DIM)[:, sp - w_keep:]))

        so = dict(zip(_PROJ_NAMES, _proj(xs, mod_s[1], mod_s[0], lw, cm_s, sm_s, tm_s)))
        per_tok = lambda a: a.reshape(bs, ss, a.shape[1])
        newk = lambda a: _pad_rows(per_tok(a), _NEW_PAD)
        qm = so["qm"].reshape(bs, ss * MLA_HEADS, 2 * LANES)
        fq = so["fq"].reshape(bs, ss * FOX_HEADS, LANES)
        nq = so["nq"].reshape(bs, ss * NSA_HEADS, LANES)
        misc8 = _pad_rows(per_tok(so["misc"]), SUBLANES)
        omla, ofox, cmpm = _decode1(l, page_table, qm, fq, newk(so["kvm"]), newk(so["kvf"]), misc8, lw["wuv"],
                                    cache_mla, cache_fox, logf_t, cache_nsa, pp)
        cmpm = cmpm.reshape(bs, n_pages * (PAGE_SIZE // CMP_BLOCK), LANES)
        cmp_eo = jnp.concatenate([cmpm[:, 0::2], cmpm[:, 1::2]], axis=1)
        onsa = _decode2(l, page_table, nq, misc8, cmp_eo, bkey, bnear, bcmp, win_buf[l],
                        newk(so["kvw"]), newk(so["kvn"][:, 2 * HEAD_DIM:]), cache_nsa, pp)
        mix_s = jnp.concatenate([omla.reshape(ns_, -1), ofox.reshape(ns_, -1), onsa.reshape(ns_, -1)],
                                axis=1).astype(BF16)
        xs = _wo(mix_s, xs, mod_s[2], lw["wo"], tm_s)
        xs = _peer_block(xs, mod_s[4], mod_s[3], mod_s[5], g_ffn[l][None], wpq, keys, tbl_u, tbl_v, tm_s, 128)
        win_new = so["win"].reshape(bs, ss, 2, HEAD_DIM)
        win_all = jnp.concatenate([state_nsa_win[l], win_new], axis=1)
        st_s.append((per_tok(so["mla"]),
                     so["fkv"].reshape(bs, ss, 2, FOX_KV_HEADS, HEAD_DIM),
                     so["misc"][:, _L_LOGF:_L_GATE].reshape(bs, ss, FOX_HEADS),
                     so["nkv"].reshape(bs, ss, 4, HEAD_DIM),
                     win_all[:, win_all.shape[1] - min(WINDOW, win_all.shape[1]):]))

    yp = _final_norm(xp, g_final[None], tm_p).reshape(bp, sp, d)
    ys = _final_norm(xs, g_final[None], tm_s).reshape(bs, ss, d)
    stack = lambda st, i: jnp.stack([s[i] for s in st])
    return (yp, ys,
            stack(st_p, 0), stack(st_s, 0), stack(st_p, 1), stack(st_s, 1), stack(st_p, 2), stack(st_s, 2),
            stack(st_p, 3), stack(st_s, 3), stack(st_p, 4), stack(st_s, 4))
```

```python
import functools
import math

import numpy as np
import jax
import jax.numpy as jnp
from jax import lax
from jax.experimental import pallas as pl
from jax.experimental.pallas import tpu as pltpu

F32 = jnp.float32
BF16 = jnp.bfloat16
I32 = jnp.int32

D_MODEL = 1024
PAGE_SIZE = 128
HEAD_DIM = 64
MLA_HEADS = 8
Q_RANK = 256
KV_RANK = 128
ROPE_DIM = 32
NOPE_DIM = 64
MLA_V_DIM = 64
ROPE_BASE = 10000.0
FOX_HEADS = 4
FOX_KV_HEADS = 2
FOX_GROUP = FOX_HEADS // FOX_KV_HEADS
NSA_HEADS = 4
CMP_BLOCK = 32
SEL_BLOCK = 64
SEL_TOPK = 16
WINDOW = 512
FORCE_SCORE = 1000.0
N_BUCKETS = 32
MAX_DISTANCE = 1024
N_KEYS = 128
N_EXPERTS = N_KEYS * N_KEYS
PEER_HEADS = 8
PEER_TOPK = 16
PEER_KEY_DIM = 128
PEER_PICKS = PEER_HEADS * PEER_TOPK
QBLK = 128
EPS = 1e-6
NEG_INF = -1e30
POS_PAD = 2 ** 30
MLA_SCALE = (NOPE_DIM + ROPE_DIM) ** -0.5
HD_SCALE = HEAD_DIM ** -0.5
MIX_WIDTH = MLA_HEADS * MLA_V_DIM + FOX_HEADS * HEAD_DIM + NSA_HEADS * HEAD_DIM
IN_SPLITS = (Q_RANK, KV_RANK, ROPE_DIM,
             FOX_HEADS * HEAD_DIM, FOX_KV_HEADS * HEAD_DIM, FOX_KV_HEADS * HEAD_DIM, FOX_HEADS,
             NSA_HEADS * HEAD_DIM, HEAD_DIM, HEAD_DIM, HEAD_DIM, HEAD_DIM, HEAD_DIM, HEAD_DIM, 3 * NSA_HEADS)
IN_WIDTH = sum(IN_SPLITS)

SUBLANES = 8
LANES = 128
VMEM_BYTES_V7X = 64 * 1024 * 1024
HALF_EXPERTS = N_EXPERTS // 2


def _cparams(sem, vmem_mb=None):
    kw = dict(dimension_semantics=sem)
    if vmem_mb is not None:
        kw["vmem_limit_bytes"] = vmem_mb * 1024 * 1024
    return pltpu.CompilerParams(**kw)


def _topk_axis0(s, iota, k, payload=None):
    vals, outs = [], []
    for _ in range(k):
        m = jnp.max(s, axis=0, keepdims=True)
        first = jnp.min(jnp.where(s == m, iota, 1e9), axis=0, keepdims=True)
        onehot = iota == first
        if payload is None:
            outs.append(first)
        else:
            outs.append(jnp.max(jnp.where(onehot, payload, -1.0), axis=0, keepdims=True))
        s = jnp.where(onehot, -jnp.inf, s)
        vals.append(m)
    return vals, outs


_CAND_ROWS = ((0, PEER_TOPK),) + tuple((a, SUBLANES) for a in range(1, SUBLANES))
_N_CAND = PEER_TOPK + (SUBLANES - 1) * SUBLANES + SUBLANES


def _peer_route_kernel(x_ref, sc_ref, sh_ref, g_ref, wpq_ref, keys_ref,
                       h_ref, idx_ref, hi_ref, gate_ref):
    x = x_ref[...]
    y = x * lax.rsqrt(jnp.mean(x * x, axis=-1, keepdims=True) + EPS) * g_ref[...]
    h = y * (1.0 + sc_ref[0]) + sh_ref[0]
    h_ref[...] = h
    q = jnp.dot(h.astype(BF16), wpq_ref[...], preferred_element_type=F32).astype(BF16)
    tm = x.shape[0]
    iota_k = lax.broadcasted_iota(I32, (N_KEYS, tm), 0).astype(F32)
    rc = lax.broadcasted_iota(I32, (_N_CAND, tm), 0)
    mid = rc - PEER_TOPK
    flat = jnp.where(rc < PEER_TOPK, rc,
                     jnp.where(rc < _N_CAND - SUBLANES,
                               (1 + mid // SUBLANES) * PEER_TOPK + (mid & (SUBLANES - 1)),
                               (rc - (_N_CAND - 2 * SUBLANES)) * PEER_TOPK))
    iota_c = flat.astype(F32)
    for head in range(PEER_HEADS):
        tops = []
        for p in range(2):
            c = (head * 2 + p) * PEER_KEY_DIM
            s = lax.dot_general(keys_ref[head * 2 + p], q[:, c:c + PEER_KEY_DIM],
                                (((1,), (1,)), ((), ())), preferred_element_type=F32)
            tops.append(_topk_axis0(s, iota_k, PEER_TOPK))
        (v1, i1), (v2, i2) = tops
        v1s, i1s = jnp.concatenate(v1, axis=0), jnp.concatenate(i1, axis=0)
        v2s, i2s = jnp.concatenate(v2, axis=0), jnp.concatenate(i2, axis=0)
        cand = jnp.concatenate([v1[a] + v2s[0:nb] for a, nb in _CAND_ROWS] + [v1s[SUBLANES:] + v2[0]], axis=0)
        cidx = jnp.concatenate([i1[a] * float(N_KEYS) + i2s[0:nb] for a, nb in _CAND_ROWS]
                               + [i1s[SUBLANES:] * float(N_KEYS) + i2[0]], axis=0)
        tv, te = _topk_axis0(cand, iota_c, PEER_TOPK, payload=cidx)
        tv = jnp.concatenate(tv, axis=0)
        te = jnp.concatenate(te, axis=0).astype(I32)
        e = jnp.exp(tv - tv[0:1])
        gate = e / jnp.sum(e, axis=0, keepdims=True)
        rows = slice(head * PEER_TOPK, (head + 1) * PEER_TOPK)
        idx_ref[0, rows, :] = (te & (HALF_EXPERTS - 1)) * SUBLANES
        hi_ref[0, rows, :] = jnp.where(te >= HALF_EXPERTS, 1.0, 0.0)
        gate_ref[0, rows, :] = gate


def _peer_route(x, sc, sh, g, wpq, keys, tm):
    n = x.shape[0]
    nblk = n // tm
    bpg = nblk // sc.shape[0]
    mod_spec = pl.BlockSpec((1,) + sc.shape[1:], lambda i: (i // bpg, 0, 0))
    pick_spec = pl.BlockSpec((1, PEER_PICKS, tm), lambda i: (i, 0, 0))
    return pl.pallas_call(
        _peer_route_kernel,
        grid=(nblk,),
        in_specs=[pl.BlockSpec((tm, D_MODEL), lambda i: (i, 0)), mod_spec, mod_spec,
                  pl.BlockSpec((1, D_MODEL), lambda i: (0, 0)),
                  pl.BlockSpec(wpq.shape, lambda i: (0, 0)),
                  pl.BlockSpec(keys.shape, lambda i: (0, 0, 0))],
        out_specs=[pl.BlockSpec((tm, D_MODEL), lambda i: (i, 0)), pick_spec, pick_spec, pick_spec],
        out_shape=[jax.ShapeDtypeStruct((n, D_MODEL), F32),
                   jax.ShapeDtypeStruct((nblk, PEER_PICKS, tm), I32),
                   jax.ShapeDtypeStruct((nblk, PEER_PICKS, tm), F32),
                   jax.ShapeDtypeStruct((nblk, PEER_PICKS, tm), F32)],
        compiler_params=_cparams(("parallel",), 48),
        name="peer_route",
    )(x, sc, sh, g, wpq, keys)


_ROWS_PER_PICK = 2 * SUBLANES
PICK_ROWS = PEER_PICKS * _ROWS_PER_PICK


def _gelu_exact(x):
    return 0.5 * x * (1.0 + lax.erf(x * (2.0 ** -0.5)))


def _pick_expand():
    p = lax.broadcasted_iota(I32, (PEER_PICKS, PICK_ROWS), 0)
    k = lax.broadcasted_iota(I32, (PEER_PICKS, PICK_ROWS), 1)
    return jnp.where(k // _ROWS_PER_PICK == p, 1.0, 0.0).astype(BF16)


def _half_matches(hi_ref, expand):
    tm = hi_ref.shape[0]
    hi_cols = _dot(hi_ref[...].astype(BF16), expand)
    k = lax.broadcasted_iota(I32, (tm, PICK_ROWS), 1)
    return hi_cols == (k & 1).astype(F32)


def _chunk_diag(rows):
    s = lax.broadcasted_iota(I32, (rows, PICK_ROWS), 0)
    k = lax.broadcasted_iota(I32, (rows, PICK_ROWS), 1)
    return s == (k % _ROWS_PER_PICK) // 2


def _gather_rows(idx_ref, tbl_ref, g_scr, t):
    tok_idx = idx_ref.at[pl.ds(t * PEER_PICKS, PEER_PICKS)]
    for p in range(PEER_PICKS):
        r = pl.multiple_of(tok_idx[p], SUBLANES)
        g_scr[p * SUBLANES:(p + 1) * SUBLANES, :] = tbl_ref[pl.ds(r, SUBLANES), :]


_GATHER_AHEAD = 2
_GATHER_BUFS = 2 * _GATHER_AHEAD


def _staged_tokens(tm, gather, compute, bufs):
    n = len(bufs)
    for j in range(_GATHER_AHEAD):
        gather(bufs[j], j)

    def group(i, c):
        t0 = n * i
        for j in range(n):
            gather(bufs[(j + _GATHER_AHEAD) % n], jnp.minimum(t0 + j + _GATHER_AHEAD, tm - 1))
            compute(bufs[j], t0 + j)
        return c

    lax.fori_loop(0, tm // n, group, 0)


def _peer_act_kernel(idx_ref, hi_ref, h_ref, gate_ref, tbl_ref, w_ref, *scratch):
    bufs, rs_scr = scratch[:_GATHER_BUFS], scratch[_GATHER_BUFS]
    tm = h_ref.shape[0]
    diag = _chunk_diag(SUBLANES)

    def compute(g_scr, t):
        vt = pltpu.bitcast(g_scr[...], BF16)
        hb = h_ref[t].astype(BF16)
        r = _dot_t(jnp.concatenate([hb, hb], axis=0), vt)[0:SUBLANES]
        rs_scr[pl.ds(t, 1), :] = jnp.sum(jnp.where(diag, r, 0.0), axis=0, keepdims=True)

    _staged_tokens(tm, functools.partial(_gather_rows, idx_ref, tbl_ref), compute, bufs)
    expand = _pick_expand()
    x1, x2, x3 = _split3(jnp.where(_half_matches(hi_ref, expand), rs_scr[...], 0.0))
    act = (_dot_t(x1, expand) + _dot_t(x2, expand)) + _dot_t(x3, expand)
    w_ref[...] = gate_ref[...] * _gelu_exact(act)


def _peer_out_kernel(idx_ref, hi_ref, w_ref, x_ref, gt_ref, tbl_ref, o_ref, *scratch):
    bufs, ws_scr = scratch[:_GATHER_BUFS], scratch[_GATHER_BUFS]
    tm = x_ref.shape[0]
    expand = _pick_expand()
    w_cols = _dot(w_ref[...].astype(BF16), expand)
    ws_scr[...] = jnp.where(_half_matches(hi_ref, expand), w_cols, 0.0)
    diag = _chunk_diag(_ROWS_PER_PICK)

    def compute(g_scr, t):
        vt = pltpu.bitcast(g_scr[...], BF16)
        wrow = jnp.broadcast_to(ws_scr[pl.ds(t, 1), :], (_ROWS_PER_PICK, PICK_ROWS))
        o_ref[t] = _dot(jnp.where(diag, wrow, 0.0).astype(BF16), vt)[0:SUBLANES]

    _staged_tokens(tm, functools.partial(_gather_rows, idx_ref, tbl_ref), compute, bufs)
    o_ref[...] = x_ref[...] + gt_ref[0] * o_ref[...]


def _idx_spec(tm):
    return pl.BlockSpec((tm * PEER_PICKS,), lambda i: (i,), memory_space=pltpu.SMEM)


def _table_spec():
    return pl.BlockSpec((HALF_EXPERTS * SUBLANES, LANES), lambda i: (0, 0), pipeline_mode=pl.Buffered(1))


def _gather_scratch():
    return [pltpu.VMEM((PEER_PICKS * SUBLANES, LANES), I32) for _ in range(_GATHER_BUFS)]


def _peer_act(idx, hi, h3, gate, tbl, tm):
    n = h3.shape[0]
    picks = pl.BlockSpec((tm, PEER_PICKS), lambda i: (i, 0))
    return pl.pallas_call(
        _peer_act_kernel,
        grid=(n // tm,),
        in_specs=[_idx_spec(tm), picks, pl.BlockSpec((tm, SUBLANES, LANES), lambda i: (i, 0, 0)), picks,
                  _table_spec()],
        out_specs=picks,
        out_shape=jax.ShapeDtypeStruct((n, PEER_PICKS), F32),
        scratch_shapes=_gather_scratch() + [pltpu.VMEM((tm, PICK_ROWS), F32)],
        compiler_params=_cparams(("parallel",), 52),
        name="peer_act",
    )(idx, hi, h3, gate, tbl)


def _peer_out(idx, hi, w, x3, gt3, tbl, tm):
    n = x3.shape[0]
    bpg = (n // tm) // gt3.shape[0]
    tok = pl.BlockSpec((tm, SUBLANES, LANES), lambda i: (i, 0, 0))
    picks = pl.BlockSpec((tm, PEER_PICKS), lambda i: (i, 0))
    gt_spec = pl.BlockSpec((1,) + gt3.shape[1:], lambda i: (i // bpg, 0, 0, 0))
    return pl.pallas_call(
        _peer_out_kernel,
        grid=(n // tm,),
        in_specs=[_idx_spec(tm), picks, picks, tok, gt_spec, _table_spec()],
        out_specs=tok,
        out_shape=jax.ShapeDtypeStruct((n, SUBLANES, LANES), F32),
        scratch_shapes=_gather_scratch() + [pltpu.VMEM((tm, PICK_ROWS), F32)],
        compiler_params=_cparams(("parallel",), 52),
        name="peer_out",
    )(idx, hi, w, x3, gt3, tbl)


def _pack_table(t):
    b = lax.bitcast_convert_type(t.astype(BF16), jnp.uint16).astype(jnp.uint32)
    packed = b[:HALF_EXPERTS] | (b[HALF_EXPERTS:] << 16)
    return lax.bitcast_convert_type(packed, I32).reshape(HALF_EXPERTS * SUBLANES, LANES)


def _picks_token_major(a):
    nblk, p, tm = a.shape
    return a.transpose(0, 2, 1).reshape(nblk * tm, p)


def _peer_block(x, sc2, sh2, gt2, g_f, wpq_bf, keys_bf, tbl_u, tbl_v, tm_route, tm_pass):
    n = x.shape[0]
    h, idx, hi, gate = _peer_route(x, sc2, sh2, g_f, wpq_bf, keys_bf, tm_route)
    idx = _picks_token_major(idx).reshape(n * PEER_PICKS)
    hi, gate = _picks_token_major(hi), _picks_token_major(gate)
    w = _peer_act(idx, hi, h.reshape(n, SUBLANES, LANES), gate, tbl_u, tm_pass)
    if gt2.shape[1] == 1:
        gt3 = gt2.reshape(gt2.shape[0], 1, SUBLANES, LANES)
    else:
        gt3 = gt2.reshape(n // tm_pass, tm_pass, SUBLANES, LANES)
    out = _peer_out(idx, hi, w, x.reshape(n, SUBLANES, LANES), gt3, tbl_v, tm_pass)
    return out.reshape(n, D_MODEL)


def _rms(x, g):
    return x * lax.rsqrt(jnp.mean(x * x, axis=-1, keepdims=True) + EPS) * g


def _dot(a, b):
    return jnp.dot(a, b, preferred_element_type=F32)


def _dot_t(a, b):
    return lax.dot_general(a, b, (((1,), (1,)), ((), ())), preferred_element_type=F32)


def _sigmoid(x):
    return 1.0 / (1.0 + jnp.exp(-x))


def _softmax_update(m_ref, l_ref, acc_ref, s, v):
    m_old = m_ref[...]
    m_new = jnp.maximum(m_old, jnp.max(s, axis=1, keepdims=True))
    a = jnp.exp(m_old - m_new)
    p = jnp.exp(s - m_new)
    l_ref[...] = a * l_ref[...] + jnp.sum(p, axis=1, keepdims=True)
    acc_ref[...] = a * acc_ref[...] + _dot(p.astype(BF16), v)
    m_ref[...] = m_new


def _softmax_init(m_ref, l_ref, acc_ref):
    m_ref[...] = jnp.full(m_ref.shape, NEG_INF, F32)
    l_ref[...] = jnp.zeros(l_ref.shape, F32)
    acc_ref[...] = jnp.zeros(acc_ref.shape, F32)


def _topk_mask_lanes(imp, lane, k):
    sel = jnp.zeros(imp.shape, F32)
    for _ in range(k):
        m = jnp.max(imp, axis=1, keepdims=True)
        first = jnp.min(jnp.where(imp == m, lane, 1e9), axis=1, keepdims=True)
        hit = lane == first
        sel = jnp.where(hit, 1.0, sel)
        imp = jnp.where(hit, -jnp.inf, imp)
    return sel


def _ada_kernel(c_ref, w_ref, b_ref, o_ref):
    c = c_ref[...]
    o_ref[...] = jnp.dot(c * _sigmoid(c), w_ref[...], preferred_element_type=F32,
                         precision=lax.Precision.HIGHEST) + b_ref[...]


def _ada(c, w, b):
    nb, d = c.shape
    n_out = w.shape[1]
    tn = 1536
    return pl.pallas_call(
        _ada_kernel,
        grid=(n_out // tn,),
        in_specs=[pl.BlockSpec((nb, d), lambda j: (0, 0)),
                  pl.BlockSpec((d, tn), lambda j: (0, j)),
                  pl.BlockSpec((1, tn), lambda j: (0, j))],
        out_specs=pl.BlockSpec((nb, tn), lambda j: (0, j)),
        out_shape=jax.ShapeDtypeStruct((nb, n_out), F32),
        compiler_params=_cparams(("parallel",), 40),
        name="ada_mod",
    )(c, w, b)


_C_CQ, _C_CKV, _C_GA, _C_GB, _C_FQ, _C_FKV, _C_NQ, _C_NKV, _C_WKV, _C_END = (
    0, 256, 384, 512, 640, 1152, 1408, 1920, 2176, 2304)
_L_LOGF = ROPE_DIM
_L_GATE = ROPE_DIM + FOX_HEADS
_L_GEND = _L_GATE + 3 * NSA_HEADS
_Q_NOPE, _Q_ROPE, _Q_RSW, _Q_END = 0, 512, 1536, 2560


def _proj_kernel(x_ref, sc_ref, sh_ref, gm_ref, w1_ref, gq_ref, gkv_ref, wq_ref, wuk_ref, bf_ref, cm_ref, sm_ref,
                 mla_ref, fkv_ref, nkv_ref, win_ref, misc_ref,
                 qm_ref, kvm_ref, fq_ref, kvf_ref, nq_ref, kvn_ref, kvw_ref, cmp_ref):
    tm = x_ref.shape[0]
    h = _rms(x_ref[...], gm_ref[...]) * (1.0 + sc_ref[0]) + sh_ref[0]
    proj = _dot(h.astype(BF16), w1_ref[...])
    cm = cm_ref[...]
    sm = sm_ref[...]
    cqn = _rms(proj[:, _C_CQ:_C_CKV], gq_ref[...])
    q2 = _dot(cqn.astype(BF16), wq_ref[...])
    qlat = _dot(q2[:, _Q_NOPE:_Q_ROPE].astype(BF16), wuk_ref[...])
    for hd in range(MLA_HEADS):
        lo = hd * LANES
        rot = q2[:, _Q_ROPE + lo:_Q_ROPE + lo + LANES] * cm + q2[:, _Q_RSW + lo:_Q_RSW + lo + LANES] * sm
        qm_ref[:, 2 * lo:2 * lo + LANES] = (qlat[:, lo:lo + LANES] * MLA_SCALE).astype(BF16)
        qm_ref[:, 2 * lo + LANES:2 * lo + 2 * LANES] = (rot * MLA_SCALE).astype(BF16)
    ckvn = _rms(proj[:, _C_CKV:_C_GA], gkv_ref[...])
    ga = proj[:, _C_GA:_C_GB]
    krot = ga * cm + proj[:, _C_GB:_C_FQ] * sm
    mla_ref[:, 0:KV_RANK] = ckvn
    mla_ref[:, KV_RANK:KV_RANK + ROPE_DIM] = krot[:, 0:ROPE_DIM]
    kvm_ref[:, 0:LANES] = ckvn.astype(BF16)
    kvm_ref[:, LANES:2 * LANES] = krot.astype(BF16)
    lane = lax.broadcasted_iota(I32, (tm, LANES), 1)
    z = ga + bf_ref[...]
    logsig = jnp.minimum(z, 0.0) - jnp.log(1.0 + jnp.exp(-jnp.abs(z)))
    misc_ref[...] = jnp.where(lane < _L_LOGF, krot,
                              jnp.where(lane < _L_GATE, logsig,
                                        jnp.where(lane < _L_GEND, _sigmoid(ga), 0.0)))
    fq_ref[...] = (proj[:, _C_FQ:_C_FKV] * HD_SCALE).astype(BF16)
    nq_ref[...] = (proj[:, _C_NQ:_C_NKV] * HD_SCALE).astype(BF16)
    fkv = proj[:, _C_FKV:_C_NQ]
    nkv = proj[:, _C_NKV:_C_WKV]
    wkv = proj[:, _C_WKV:_C_END]
    fkv_ref[...] = fkv
    nkv_ref[...] = nkv
    win_ref[...] = wkv
    kvf_ref[...] = fkv.astype(BF16)
    kvn_ref[...] = nkv.astype(BF16)
    kvw_ref[...] = wkv.astype(BF16)
    cmp_ref[...] = jnp.sum(nkv[:, 0:LANES].reshape(tm // CMP_BLOCK, CMP_BLOCK, LANES), axis=1) * (1.0 / CMP_BLOCK)


def _proj(x, sc, sh, lw, cm, sm, tm):
    n = x.shape[0]
    nblk = n // tm
    bpg = nblk // sc.shape[0]
    pblk = cm.shape[0] // tm
    mod_spec = pl.BlockSpec((1,) + sc.shape[1:], lambda i: (i // bpg, 0, 0))
    const = lambda a: pl.BlockSpec(a.shape, lambda i: (0,) * a.ndim)
    tok = lambda w: pl.BlockSpec((tm, w), lambda i: (i, 0))
    rot_spec = pl.BlockSpec((tm, LANES), lambda i: (i % pblk, 0))
    widths = [(KV_RANK + ROPE_DIM, F32), (2 * LANES, F32), (2 * LANES, F32), (LANES, F32), (LANES, F32),
              (MLA_HEADS * 2 * LANES, BF16), (2 * LANES, BF16), (FOX_HEADS * LANES, BF16), (2 * LANES, BF16),
              (NSA_HEADS * LANES, BF16), (2 * LANES, BF16), (LANES, BF16)]
    out_specs = [tok(w) for w, _ in widths] + [pl.BlockSpec((tm // CMP_BLOCK, LANES), lambda i: (i, 0))]
    out_shape = [jax.ShapeDtypeStruct((n, w), dt) for w, dt in widths] + [
        jax.ShapeDtypeStruct((n // CMP_BLOCK, LANES), F32)]
    return pl.pallas_call(
        _proj_kernel,
        grid=(nblk,),
        in_specs=[tok(D_MODEL), mod_spec, mod_spec, const(lw["g_mix"]), const(lw["w1"]), const(lw["g_q"]),
                  const(lw["g_kv"]), const(lw["wq"]), const(lw["wuk"]), const(lw["bf"]), rot_spec, rot_spec],
        out_specs=out_specs,
        out_shape=out_shape,
        compiler_params=_cparams(("parallel",), 48),
        name="in_proj",
    )(x, sc, sh, lw["g_mix"], lw["w1"], lw["g_q"], lw["g_kv"], lw["wq"], lw["wuk"], lw["bf"], cm, sm)


def _split3(x):
    x1 = x.astype(BF16)
    r = x - x1.astype(F32)
    x2 = r.astype(BF16)
    x3 = (r - x2.astype(F32)).astype(BF16)
    return x1, x2, x3


def _cum_kernel(misc_ref, col_ref, row_ref):
    nblk = misc_ref.shape[1] // LANES
    ri = lax.broadcasted_iota(I32, (LANES, LANES), 0)
    ci = lax.broadcasted_iota(I32, (LANES, LANES), 1)
    tri = jnp.where(ci <= ri, 1.0, 0.0).astype(BF16)
    carry = jnp.zeros((1, LANES), F32)
    for blk in range(nblk):
        x1, x2, x3 = _split3(misc_ref[0, blk * LANES:(blk + 1) * LANES, :])
        c = (_dot(tri, x1) + _dot(tri, x2)) + _dot(tri, x3) + carry
        carry = c[LANES - 1:LANES, :]
        col_ref[0, blk * LANES:(blk + 1) * LANES, :] = c
        row_ref[0, blk] = c.T[_L_LOGF:_L_LOGF + SUBLANES, :]


def _cum(misc3):
    b, s, _ = misc3.shape
    return pl.pallas_call(
        _cum_kernel,
        grid=(b,),
        in_specs=[pl.BlockSpec((1, s, LANES), lambda i: (i, 0, 0))],
        out_specs=[pl.BlockSpec((1, s, LANES), lambda i: (i, 0, 0)),
                   pl.BlockSpec((1, s // LANES, SUBLANES, LANES), lambda i: (i, 0, 0, 0))],
        out_shape=[jax.ShapeDtypeStruct((b, s, LANES), F32),
                   jax.ShapeDtypeStruct((b, s // LANES, SUBLANES, LANES), F32)],
        compiler_params=_cparams(("parallel",)),
        name="fox_cumsum",
    )(misc3)


_T5_EXACT = N_BUCKETS // 2
_T5_THRESH = tuple(int(math.ceil(_T5_EXACT * (MAX_DISTANCE / _T5_EXACT) ** (j / (N_BUCKETS - _T5_EXACT)) - 1e-9))
                   for j in range(1, N_BUCKETS - _T5_EXACT))


def _t5_bias(tbl_ref, dist):
    n = jnp.maximum(dist, 0)
    big = jnp.full(n.shape, _T5_EXACT, I32)
    for t in _T5_THRESH:
        big = big + jnp.where(n >= t, 1, 0)
    bucket = jnp.where(n < _T5_EXACT, n, big)
    outs = [jnp.zeros(n.shape, F32) for _ in range(NSA_HEADS)]
    for j in range(N_BUCKETS):
        hit = bucket == j
        for h in range(NSA_HEADS):
            outs[h] = jnp.where(hit, tbl_ref[j, h], outs[h])
    return outs


def _cmp_block_of_lane(lane, half):
    return jnp.where(lane < half, 2 * lane, 2 * (lane - half) + 1)


def _t5_prompt_kernel(tbl_ref, toep_ref, cmpb_ref):
    nq = toep_ref.shape[1]
    ncmp = cmpb_ref.shape[3]
    i = lax.broadcasted_iota(I32, (QBLK, QBLK), 0)
    j = lax.broadcasted_iota(I32, (QBLK, QBLK), 1)
    ic = lax.broadcasted_iota(I32, (QBLK, ncmp), 0)
    lc = lax.broadcasted_iota(I32, (QBLK, ncmp), 1)
    cmp_end = _cmp_block_of_lane(lc, ncmp // 2) * CMP_BLOCK + (CMP_BLOCK - 1)
    for off in range(nq):
        for h, v in enumerate(_t5_bias(tbl_ref, off * QBLK + i - j)):
            toep_ref[h, off] = v
        for h, v in enumerate(_t5_bias(tbl_ref, off * QBLK + ic - cmp_end)):
            cmpb_ref[h, off] = v


def _t5_prompt(t5_table, s):
    nq = s // QBLK
    ncmp = s // CMP_BLOCK
    return pl.pallas_call(
        _t5_prompt_kernel,
        in_specs=[pl.BlockSpec(memory_space=pltpu.SMEM)],
        out_shape=[jax.ShapeDtypeStruct((NSA_HEADS, nq, QBLK, QBLK), F32),
                   jax.ShapeDtypeStruct((NSA_HEADS, nq, QBLK, ncmp), F32)],
        compiler_params=pltpu.CompilerParams(vmem_limit_bytes=40 * 1024 * 1024),
        name="t5_prompt_tables",
    )(t5_table)


def _t5_decode_kernel(tbl_ref, key_ref, near_ref, cmpb_ref, *, past_len, n_new):
    lp = key_ref.shape[2]
    ncmp = cmpb_ref.shape[2]
    wl = near_ref.shape[2]
    qi = lambda w: lax.broadcasted_iota(I32, (SUBLANES, w), 0)
    ln = lambda w: lax.broadcasted_iota(I32, (SUBLANES, w), 1)
    for h, v in enumerate(_t5_bias(tbl_ref, past_len + qi(lp) - ln(lp))):
        key_ref[h] = v
    for h, v in enumerate(_t5_bias(tbl_ref, WINDOW + qi(wl) - ln(wl))):
        near_ref[h] = v
    cmp_end = _cmp_block_of_lane(ln(ncmp), ncmp // 2) * CMP_BLOCK + (CMP_BLOCK - 1)
    for h, v in enumerate(_t5_bias(tbl_ref, past_len + qi(ncmp) - cmp_end)):
        cmpb_ref[h] = v


def _t5_decode(t5_table, past_len, n_new, w_buf):
    del w_buf
    return pl.pallas_call(
        functools.partial(_t5_decode_kernel, past_len=past_len, n_new=n_new),
        in_specs=[pl.BlockSpec(memory_space=pltpu.SMEM)],
        out_shape=[jax.ShapeDtypeStruct((NSA_HEADS, SUBLANES, past_len), F32),
                   jax.ShapeDtypeStruct((NSA_HEADS, SUBLANES, WINDOW + LANES), F32),
                   jax.ShapeDtypeStruct((NSA_HEADS, SUBLANES, past_len // CMP_BLOCK), F32)],
        compiler_params=pltpu.CompilerParams(vmem_limit_bytes=40 * 1024 * 1024),
        name="t5_decode_tables",
    )(t5_table)


def _attn_prompt_kernel(qm_ref, fq_ref, nq_ref, cumc_ref, misc_ref, cmpb_ref,
                        kvm_ref, kvf_ref, kvn_ref, kvw_ref, cmp_ref, cumr_ref, toep_ref, wuv_ref, eblk_ref,
                        mix_ref,
                        q8_scr, fq_scr, nq_scr, cq_scr, selk_scr, m_scr, l_scr, acc_scr,
                        fm_scr, fl_scr, fa_scr, sm_scr, sl_scr, sa_scr):
    qi = pl.program_id(1)
    s_len = kvm_ref.shape[1]
    n_kb = s_len // QBLK
    n_cmp = s_len // CMP_BLOCK
    n_sel = s_len // SEL_BLOCK
    top = min(SEL_TOPK, n_sel)
    q0 = qi * QBLK

    def rows_pos(nh):
        r = lax.broadcasted_iota(I32, (nh * QBLK, QBLK), 0)
        return q0 + (r & (QBLK - 1))

    def key_pos(nh, kb):
        return kb * QBLK + lax.broadcasted_iota(I32, (nh * QBLK, QBLK), 1)

    for h in range(MLA_HEADS):
        q8_scr[h * QBLK:(h + 1) * QBLK, :] = qm_ref[:, h * 2 * LANES:(h + 1) * 2 * LANES]
    cumc = cumc_ref[0]
    for h in range(FOX_HEADS):
        fq_scr[h * QBLK:(h + 1) * QBLK, :] = fq_ref[:, h * LANES:(h + 1) * LANES]
        cq_scr[h * QBLK:(h + 1) * QBLK, :] = jnp.broadcast_to(cumc[:, _L_LOGF + h:_L_LOGF + h + 1], (QBLK, LANES))
    for h in range(NSA_HEADS):
        nq_scr[h * QBLK:(h + 1) * QBLK, :] = nq_ref[:, h * LANES:(h + 1) * LANES]

    _softmax_init(m_scr, l_scr, acc_scr)
    qp8 = rows_pos(MLA_HEADS)

    def mla_step(kb, diagonal):
        k = kvm_ref[0, pl.ds(pl.multiple_of(kb * QBLK, QBLK), QBLK), :]
        s = _dot_t(q8_scr[...], k)
        if diagonal:
            s = jnp.where(key_pos(MLA_HEADS, kb) <= qp8, s, NEG_INF)
        _softmax_update(m_scr, l_scr, acc_scr, s, k[:, 0:LANES])

    nr = FOX_HEADS * QBLK
    _softmax_init(fm_scr, fl_scr, fa_scr)
    qp4 = rows_pos(FOX_HEADS)

    def fox_step(kb, diagonal):
        k = kvf_ref[0, pl.ds(pl.multiple_of(kb * QBLK, QBLK), QBLK), :]
        ck = cumr_ref[0, kb]
        ck4 = jnp.concatenate([jnp.broadcast_to(ck[h:h + 1, :], (QBLK, LANES)) for h in range(FOX_HEADS)], axis=0)
        s = _dot_t(fq_scr[...], k[:, 0:LANES]) + (cq_scr[...] - ck4)
        if diagonal:
            s = jnp.where(key_pos(FOX_HEADS, kb) <= qp4, s, NEG_INF)
        _softmax_update(fm_scr, fl_scr, fa_scr, s, k[:, LANES:2 * LANES])

    half = n_cmp // 2
    cmpk = jnp.concatenate([cmp_ref[0, pl.ds(0, half, stride=2), :], cmp_ref[0, pl.ds(1, half, stride=2), :]],
                           axis=0).astype(BF16)
    nqv = nq_scr[...]
    bias_c = jnp.concatenate([cmpb_ref[h, 0] for h in range(NSA_HEADS)], axis=0)
    lane_c = lax.broadcasted_iota(I32, (nr, n_cmp), 1)
    cmp_end = _cmp_block_of_lane(lane_c, half) * CMP_BLOCK + (CMP_BLOCK - 1)
    qpc = q0 + (lax.broadcasted_iota(I32, (nr, n_cmp), 0) & (QBLK - 1))
    valid_c = cmp_end <= qpc
    s = jnp.where(valid_c, _dot_t(nqv, cmpk) + bias_c, NEG_INF)
    p = jnp.exp(s - jnp.max(s, axis=1, keepdims=True))
    pc = jnp.where(valid_c, p / jnp.sum(p, axis=1, keepdims=True), 0.0)
    o_cmp = _dot(pc.astype(BF16), cmpk)

    pcs = pc[0:QBLK]
    for h in range(1, NSA_HEADS):
        pcs = pcs + pc[h * QBLK:(h + 1) * QBLK]
    imp = pcs[:, 0:half] + pcs[:, half:n_cmp]
    imp_t = jnp.concatenate([imp, jnp.zeros((QBLK, LANES - n_sel), F32)], axis=1).T[0:n_sel, :]
    blk = lax.broadcasted_iota(I32, (n_sel, QBLK), 0)
    qps = q0 + lax.broadcasted_iota(I32, (n_sel, QBLK), 1)
    forced = (blk == qps // SEL_BLOCK) | (blk == 0)
    imp_t = jnp.where(forced, FORCE_SCORE, imp_t)
    imp_t = jnp.where(blk * SEL_BLOCK > qps, -1.0, imp_t)
    rank = jnp.zeros((n_sel, QBLK), F32)
    for i in range(n_sel):
        row = imp_t[i:i + 1, :]
        rank = rank + jnp.where((row > imp_t) | ((row == imp_t) & (blk > i)), 1.0, 0.0)
    sel_t = jnp.where(rank < top, 1.0, 0.0)
    sel = jnp.concatenate([sel_t, jnp.zeros((LANES - n_sel, QBLK), F32)], axis=0).T
    selk = _dot(sel.astype(BF16), eblk_ref[...])
    for kb in range(n_kb):
        selk_scr[kb] = selk[:, kb * QBLK:(kb + 1) * QBLK]

    def toep4(off):
        return jnp.concatenate([toep_ref[h, off] for h in range(NSA_HEADS)], axis=0)

    _softmax_init(sm_scr, sl_scr, sa_scr)

    def sel_step(kb, diagonal):
        k = kvn_ref[0, pl.ds(pl.multiple_of(kb * QBLK, QBLK), QBLK), :][:, LANES:2 * LANES]
        s = _dot_t(nq_scr[...], k) + toep4(qi - kb)
        mk = selk_scr[kb]
        keep = jnp.concatenate([mk] * NSA_HEADS, axis=0) > 0.5
        if diagonal:
            keep = keep & (key_pos(NSA_HEADS, kb) <= qp4)
        _softmax_update(sm_scr, sl_scr, sa_scr, jnp.where(keep, s, NEG_INF), k)

    def causal_body(kb, c):
        mla_step(kb, False)
        fox_step(kb, False)
        sel_step(kb, False)
        return c

    lax.fori_loop(0, qi, causal_body, 0)
    mla_step(qi, True)
    fox_step(qi, True)
    sel_step(qi, True)

    o_lat = (acc_scr[...] / l_scr[...]).astype(BF16)
    o_mla = _dot(o_lat[0:QBLK], wuv_ref[0])
    for h in range(1, MLA_HEADS):
        o_mla = o_mla + _dot(o_lat[h * QBLK:(h + 1) * QBLK], wuv_ref[h])
    mix_ref[:, 0:MLA_HEADS * MLA_V_DIM] = o_mla.astype(BF16)
    o_fox = fa_scr[...] / fl_scr[...]
    base = MLA_HEADS * MLA_V_DIM
    for h in range(FOX_HEADS):
        mix_ref[:, base + h * LANES:base + (h + 1) * LANES] = o_fox[h * QBLK:(h + 1) * QBLK].astype(BF16)
    o_sel = sa_scr[...] / sl_scr[...]

    m_r, l_r, a_r = fm_scr, fl_scr, fa_scr
    _softmax_init(m_r, l_r, a_r)

    def win_step(kb, c):
        k = kvw_ref[0, pl.ds(pl.multiple_of(kb * QBLK, QBLK), QBLK), :]
        s = _dot_t(nq_scr[...], k) + toep4(qi - kb)
        dist = qp4 - key_pos(NSA_HEADS, kb)
        s = jnp.where((dist >= 0) & (dist <= WINDOW), s, NEG_INF)
        _softmax_update(m_r, l_r, a_r, s, k)
        return c

    lax.fori_loop(jnp.maximum(qi - WINDOW // QBLK, 0), qi + 1, win_step, 0)
    o_win = a_r[...] / l_r[...]

    misc = misc_ref[...]
    base = MLA_HEADS * MLA_V_DIM + FOX_HEADS * LANES
    for h in range(NSA_HEADS):
        rows = slice(h * QBLK, (h + 1) * QBLK)
        gate = lambda c: misc[:, _L_GATE + c * NSA_HEADS + h:_L_GATE + c * NSA_HEADS + h + 1]
        o = gate(0) * o_cmp[rows] + gate(1) * o_sel[rows] + gate(2) * o_win[rows]
        mix_ref[:, base + h * LANES:base + (h + 1) * LANES] = o.astype(BF16)


_MIX_PAD = MLA_HEADS * MLA_V_DIM + (FOX_HEADS + NSA_HEADS) * LANES


def _attn_prompt(po, cumc, cumr, toep, cmpb, wuv, b, s):
    nq = s // QBLK
    qblk = lambda w: pl.BlockSpec((QBLK, w), lambda bi, qi: (bi * nq + qi, 0))
    per_b = lambda a: pl.BlockSpec((1,) + a.shape[1:], lambda bi, qi: (bi,) + (0,) * (a.ndim - 1))
    const = lambda a: pl.BlockSpec(a.shape, lambda bi, qi: (0,) * a.ndim)
    kvm = po["kvm"].reshape(b, s, 2 * LANES)
    kvf = po["kvf"].reshape(b, s, 2 * LANES)
    kvn = po["kvn"].reshape(b, s, 2 * LANES)
    kvw = po["kvw"].reshape(b, s, LANES)
    cmpm = po["cmp"].reshape(b, s // CMP_BLOCK, LANES)
    rows = MLA_HEADS * QBLK
    assert s // SEL_BLOCK <= LANES
    eblk = (jnp.arange(s, dtype=I32)[None, :] // SEL_BLOCK == jnp.arange(LANES, dtype=I32)[:, None]).astype(BF16)
    return pl.pallas_call(
        _attn_prompt_kernel,
        grid=(b, nq),
        in_specs=[qblk(MLA_HEADS * 2 * LANES), qblk(FOX_HEADS * LANES), qblk(NSA_HEADS * LANES),
                  pl.BlockSpec((1, QBLK, LANES), lambda bi, qi: (bi, qi, 0)), qblk(LANES),
                  pl.BlockSpec((NSA_HEADS, 1, QBLK, s // CMP_BLOCK), lambda bi, qi: (0, qi, 0, 0)),
                  per_b(kvm), per_b(kvf), per_b(kvn), per_b(kvw), per_b(cmpm), per_b(cumr),
                  const(toep), const(wuv), const(eblk)],
        out_specs=qblk(_MIX_PAD),
        out_shape=jax.ShapeDtypeStruct((b * s, _MIX_PAD), BF16),
        scratch_shapes=[pltpu.VMEM((rows, 2 * LANES), BF16),
                        pltpu.VMEM((FOX_HEADS * QBLK, LANES), BF16),
                        pltpu.VMEM((NSA_HEADS * QBLK, LANES), BF16),
                        pltpu.VMEM((FOX_HEADS * QBLK, LANES), F32),
                        pltpu.VMEM((nq, QBLK, QBLK), F32),
                        pltpu.VMEM((rows, LANES), F32), pltpu.VMEM((rows, LANES), F32),
                        pltpu.VMEM((rows, LANES), F32)]
                       + [pltpu.VMEM((FOX_HEADS * QBLK, LANES), F32) for _ in range(6)],
        compiler_params=_cparams(("parallel", "arbitrary"), 48),
        name="attn_prompt",
    )(po["qm"], po["fq"], po["nq"], cumc, po["misc"], cmpb, kvm, kvf, kvn, kvw, cmpm, cumr, toep, wuv, eblk)


def _wo_kernel(mix_ref, x_ref, gt_ref, w_ref, o_ref):
    o_ref[...] = x_ref[...] + gt_ref[0] * _dot(mix_ref[...], w_ref[...])


def _wo(mix, x, gt, w, tm):
    n = x.shape[0]
    nblk = n // tm
    bpg = nblk // gt.shape[0]
    return pl.pallas_call(
        _wo_kernel,
        grid=(nblk,),
        in_specs=[pl.BlockSpec((tm, mix.shape[1]), lambda i: (i, 0)),
                  pl.BlockSpec((tm, D_MODEL), lambda i: (i, 0)),
                  pl.BlockSpec((1,) + gt.shape[1:], lambda i: (i // bpg, 0, 0)),
                  pl.BlockSpec(w.shape, lambda i: (0, 0))],
        out_specs=pl.BlockSpec((tm, D_MODEL), lambda i: (i, 0)),
        out_shape=jax.ShapeDtypeStruct((n, D_MODEL), F32),
        compiler_params=_cparams(("parallel",), 40),
        name="out_proj",
    )(mix, x, gt, w)


def _final_norm_kernel(x_ref, g_ref, o_ref):
    o_ref[...] = _rms(x_ref[...], g_ref[...])


def _final_norm(x, g, tm):
    n = x.shape[0]
    return pl.pallas_call(
        _final_norm_kernel,
        grid=(n // tm,),
        in_specs=[pl.BlockSpec((tm, D_MODEL), lambda i: (i, 0)), pl.BlockSpec((1, D_MODEL), lambda i: (0, 0))],
        out_specs=pl.BlockSpec((tm, D_MODEL), lambda i: (i, 0)),
        out_shape=jax.ShapeDtypeStruct((n, D_MODEL), F32),
        compiler_params=_cparams(("parallel",)),
        name="final_norm",
    )(x, g)


_NEW_PAD = LANES


def _rows_from_tokens(tok, n_rows, per):
    r = lax.broadcasted_iota(I32, (n_rows, LANES), 0)
    out = jnp.zeros((n_rows, LANES), F32)
    for i in range(n_rows // per):
        out = jnp.where(r // per == i, jnp.broadcast_to(tok[i:i + 1, :], (n_rows, LANES)), out)
    return out


def _rep_heads(x, row0, n_rows):
    r = lax.broadcasted_iota(I32, (n_rows, LANES), 0) & (FOX_HEADS - 1)
    out = jnp.zeros((n_rows, LANES), F32)
    for h in range(FOX_HEADS):
        out = jnp.where(r == h, jnp.broadcast_to(x[row0 + h:row0 + h + 1, :], (n_rows, LANES)), out)
    return out


def _softmax_update_wide(m_ref, l_ref, acc_ref, s, vt):
    m_old = m_ref[...]
    m_new = jnp.maximum(m_old, jnp.max(s, axis=1, keepdims=True))
    a = jnp.exp(m_old - m_new)
    p = jnp.exp(s - m_new[:, 0:1])
    l_ref[...] = a * l_ref[...] + jnp.sum(p, axis=1, keepdims=True)
    acc_ref[...] = a * acc_ref[...] + _dot_t(p.astype(BF16), vt)
    m_ref[...] = m_new


def _pick_lane(x, lane_of_row):
    lane = lax.broadcasted_iota(I32, x.shape, 1)
    return jnp.sum(jnp.where(lane == lane_of_row, x, 0.0), axis=1, keepdims=True)


def _decode1_kernel(pt_ref, qm_ref, fq_ref, newm_ref, newf_ref, cnew_ref, wuv_ref, *rest, pp, n_new):
    del pt_ref
    pages = rest[:4 * pp]
    omla_ref, ofox_ref, cmpm_ref = rest[4 * pp:4 * pp + 3]
    m1, l1, a1, m2, l2, a2, suf, newc = rest[4 * pp + 3:]
    jj = pl.program_id(1)
    r_m = MLA_HEADS * n_new
    r_f = FOX_HEADS * n_new
    row_f = lax.broadcasted_iota(I32, (r_f, LANES), 0)
    lane_f = lax.broadcasted_iota(I32, (r_f, LANES), 1)
    head_lane = _L_LOGF + (row_f & (FOX_HEADS - 1))

    @pl.when(jj == 0)
    def _():
        _softmax_init(m1, l1, a1)
        _softmax_init(m2, l2, a2)
        suf[...] = jnp.zeros(suf.shape, F32)
        x = cnew_ref[0]
        sub = lax.broadcasted_iota(I32, (SUBLANES, LANES), 0)
        y = x + jnp.where(sub >= 1, pltpu.roll(x, 1, 0), 0.0)
        y = y + jnp.where(sub >= 2, pltpu.roll(y, 2, 0), 0.0)
        newc[0:SUBLANES, :] = y
        col = _pick_lane(_rows_from_tokens(y, r_f, FOX_HEADS), head_lane)
        newc[SUBLANES:SUBLANES + r_f, :] = jnp.broadcast_to(col, (r_f, LANES))

    q = qm_ref[0]
    fq = fq_ref[0]
    ri = lax.broadcasted_iota(I32, (LANES, LANES), 0)
    ci_ = lax.broadcasted_iota(I32, (LANES, LANES), 1)
    later = jnp.where(ri > ci_, 1.0, 0.0).astype(BF16)
    blocks_per_page = PAGE_SIZE // CMP_BLOCK
    ra = lax.broadcasted_iota(I32, (2 * SUBLANES, LANES), 0)
    la = lax.broadcasted_iota(I32, (2 * SUBLANES, LANES), 1)
    in_block = jnp.where((la // CMP_BLOCK == ra) & (ra < blocks_per_page), 1.0, 0.0).astype(BF16)
    c_t, kr_t, kf_t, vf_t, lfs = [], [], [], [], []
    for i in range(pp):
        pm, pf, plf, pn = pages[4 * i:4 * i + 4]
        km = pm[0, 0]
        c_t.append(km[0:KV_RANK].astype(BF16))
        kr_t.append(km[KV_RANK:KV_RANK + ROPE_DIM].astype(BF16))
        kf = pf[0, 0]
        kf_t.append(kf[0:LANES].astype(BF16))
        vf_t.append(kf[LANES:2 * LANES].astype(BF16))
        lfs.append(plf[0, 0])
        kc = pn[0, 0]
        k1 = kc.astype(BF16)
        k2 = (kc - k1.astype(F32)).astype(BF16)
        means = (_dot_t(in_block, k1) + _dot_t(in_block, k2))[0:blocks_per_page] * (1.0 / CMP_BLOCK)
        cmpm_ref[0, pp - 1 - i] = means
    c_all = jnp.concatenate(c_t, axis=1)
    s = _dot(q[:, 0:KV_RANK], c_all) + _dot(q[:, KV_RANK:KV_RANK + ROPE_DIM], jnp.concatenate(kr_t, axis=1))
    _softmax_update_wide(m1, l1, a1, s, c_all)
    lf_all = jnp.concatenate(lfs, axis=0)
    n_lf = lf_all.shape[0]
    if n_lf % (2 * SUBLANES):
        lf_all = jnp.concatenate([lf_all, jnp.zeros((2 * SUBLANES - n_lf % (2 * SUBLANES), LANES), F32)], axis=0)
    x1, x2, x3 = _split3(lf_all)
    exc_all = (_dot(x1, later) + _dot(x2, later)) + _dot(x3, later)
    tot_all = jnp.broadcast_to(exc_all[:, 0:1] + lf_all[:, 0:1], exc_all.shape)
    after = suf[...]
    new_part = newc[SUBLANES:SUBLANES + r_f, :]
    decays = []
    for i in range(pp):
        decays.append(_rep_heads(exc_all, i * FOX_HEADS, r_f) + (after + new_part))
        after = after + _rep_heads(tot_all, i * FOX_HEADS, r_f)
    suf[...] = after
    s = _dot(fq, jnp.concatenate(kf_t, axis=1)) + jnp.concatenate(decays, axis=1)
    _softmax_update_wide(m2, l2, a2, s, jnp.concatenate(vf_t, axis=1))

    @pl.when(jj == pl.num_programs(1) - 1)
    def _():
        lane_m = lax.broadcasted_iota(I32, (r_m, _NEW_PAD), 1)
        row_m = lax.broadcasted_iota(I32, (r_m, _NEW_PAD), 0)
        kn = newm_ref[0]
        s = jnp.where((lane_m <= row_m // MLA_HEADS) & (lane_m < n_new), _dot_t(q, kn), NEG_INF)
        _softmax_update(m1, l1, a1, s, kn[:, 0:KV_RANK])
        o_lat = (a1[...] / l1[...]).astype(BF16)
        rr = lax.broadcasted_iota(I32, (r_m, LANES), 0)
        o = jnp.zeros((r_m, MLA_HEADS * MLA_V_DIM), F32)
        for h in range(MLA_HEADS):
            o = o + _dot(jnp.where((rr & (MLA_HEADS - 1)) == h, o_lat, jnp.zeros_like(o_lat)), wuv_ref[h])
        omla_ref[0] = jnp.sum(o.reshape(n_new, MLA_HEADS, MLA_HEADS * MLA_V_DIM), axis=1)

        kf = newf_ref[0]
        yc = newc[0:SUBLANES, :]
        ci = newc[SUBLANES:SUBLANES + r_f, :]
        lane_n = lax.broadcasted_iota(I32, (r_f, _NEW_PAD), 1)
        row_n = lax.broadcasted_iota(I32, (r_f, _NEW_PAD), 0)
        decay = jnp.zeros((r_f, _NEW_PAD), F32)
        for j in range(n_new):
            cj = _pick_lane(jnp.broadcast_to(yc[j:j + 1, :], (r_f, LANES)), head_lane)
            decay = jnp.where(lane_n == j, ci[:, 0:_NEW_PAD] - cj, decay)
        s = _dot_t(fq, kf[:, 0:LANES]) + decay
        s = jnp.where((lane_n <= row_n // FOX_HEADS) & (lane_n < n_new), s, NEG_INF)
        _softmax_update(m2, l2, a2, s, kf[:, LANES:2 * LANES])
        ofox_ref[0] = a2[...] / l2[...]


def _decode1(layer, page_table, qm, fq, newm, newf, cnew, wuv, cache_mla, cache_fox, logf_t, cache_nsa, pp):
    b, n_pages = page_table.shape
    n_new = qm.shape[1] // MLA_HEADS
    steps = n_pages // pp
    per_b = lambda a: pl.BlockSpec((1,) + a.shape[1:], lambda bi, jj, pt: (bi,) + (0,) * (a.ndim - 1))

    def page_spec(feats, i):
        def imap(bi, jj, pt):
            return (layer, pt[bi, n_pages - 1 - (jj * pp + i)], 0, 0)
        return pl.BlockSpec((1, 1, feats, PAGE_SIZE), imap)

    in_specs = [per_b(qm), per_b(fq), per_b(newm), per_b(newf), per_b(cnew),
                pl.BlockSpec(wuv.shape, lambda bi, jj, pt: (0, 0, 0))]
    args = [qm, fq, newm, newf, cnew, wuv]
    for i in range(pp):
        in_specs += [page_spec(KV_RANK + ROPE_DIM, i), page_spec(2 * LANES, i),
                     page_spec(FOX_HEADS, i), page_spec(LANES, i)]
        args += [cache_mla, cache_fox, logf_t, cache_nsa]
    r_m, r_f = MLA_HEADS * n_new, FOX_HEADS * n_new
    grid_spec = pltpu.PrefetchScalarGridSpec(
        num_scalar_prefetch=1, grid=(b, steps), in_specs=in_specs,
        out_specs=[pl.BlockSpec((1, n_new, MLA_HEADS * MLA_V_DIM), lambda bi, jj, pt: (bi, 0, 0)),
                   pl.BlockSpec((1, r_f, LANES), lambda bi, jj, pt: (bi, 0, 0)),
                   pl.BlockSpec((1, pp, PAGE_SIZE // CMP_BLOCK, LANES), lambda bi, jj, pt: (bi, steps - 1 - jj, 0, 0))],
        scratch_shapes=[pltpu.VMEM((r_m, LANES), F32), pltpu.VMEM((r_m, LANES), F32), pltpu.VMEM((r_m, LANES), F32),
                        pltpu.VMEM((r_f, LANES), F32), pltpu.VMEM((r_f, LANES), F32), pltpu.VMEM((r_f, LANES), F32),
                        pltpu.VMEM((r_f, LANES), F32), pltpu.VMEM((SUBLANES + r_f, LANES), F32)])
    return pl.pallas_call(
        functools.partial(_decode1_kernel, pp=pp, n_new=n_new),
        grid_spec=grid_spec,
        out_shape=[jax.ShapeDtypeStruct((b, n_new, MLA_HEADS * MLA_V_DIM), F32),
                   jax.ShapeDtypeStruct((b, r_f, LANES), F32),
                   jax.ShapeDtypeStruct((b, n_pages, PAGE_SIZE // CMP_BLOCK, LANES), F32)],
        compiler_params=_cparams(("parallel", "arbitrary"), 40),
        name="decode_mla_fox",
    )(page_table, *args)


def _decode2_kernel(pt_ref, nq_ref, misc_ref, cmp_ref, bkey_ref, bnear_ref, bcmp_ref, win_ref, neww_ref, newn_ref,
                    eblk_ref, *rest, pp, n_new):
    del pt_ref
    pages = rest[:pp]
    onsa_ref = rest[pp]
    selk, ocmp, m, l, acc = rest[pp + 1:]
    jj = pl.program_id(1)
    r_n = NSA_HEADS * n_new
    nq = nq_ref[0]
    row = lax.broadcasted_iota(I32, (r_n, LANES), 0)
    lane = lax.broadcasted_iota(I32, (r_n, LANES), 1)

    @pl.when(jj == 0)
    def _():
        cmpk = cmp_ref[0].astype(BF16)
        s = _dot_t(nq, cmpk) + bcmp_ref[...]
        p = jnp.exp(s - jnp.max(s, axis=1, keepdims=True))
        pc = p / jnp.sum(p, axis=1, keepdims=True)
        ocmp[...] = _dot(pc.astype(BF16), cmpk)
        gr = lax.broadcasted_iota(I32, (r_n, r_n), 0) // NSA_HEADS
        gc = lax.broadcasted_iota(I32, (r_n, r_n), 1) // NSA_HEADS
        same_q = jnp.where(gr == gc, 1.0, 0.0).astype(BF16)
        p1, p2, p3 = _split3(pc)
        pcs = (_dot(same_q, p1) + _dot(same_q, p2)) + _dot(same_q, p3)
        half = pcs.shape[1] // 2
        imp = pcs[:, 0:half] + pcs[:, half:2 * half]
        lane_b = lax.broadcasted_iota(I32, imp.shape, 1)
        imp = jnp.where(lane_b == 0, FORCE_SCORE, imp)
        n_sel = half + 1
        sel = _topk_mask_lanes(imp, lane_b.astype(F32), min(SEL_TOPK, n_sel) - 1)
        if half < LANES:
            sel = jnp.concatenate([sel, jnp.zeros((r_n, LANES - half), F32)], axis=1)
        key_sel = _dot(sel.astype(BF16), eblk_ref[...])
        step_w = pp * LANES
        for st in range(selk.shape[0]):
            selk[st] = key_sel[:, st * step_w:(st + 1) * step_w]
        _softmax_init(m, l, acc)

    k_all = jnp.concatenate([pages[i][0, 0].astype(BF16) for i in range(pp)], axis=1)
    s = _dot(nq, k_all) + bkey_ref[...]
    s = jnp.where(selk[jj] > 0.5, s, NEG_INF)
    _softmax_update_wide(m, l, acc, s, k_all)

    @pl.when(jj == pl.num_programs(1) - 1)
    def _():
        lane_n = lax.broadcasted_iota(I32, (r_n, _NEW_PAD), 1)
        row_n = lax.broadcasted_iota(I32, (r_n, _NEW_PAD), 0)
        own = (lane_n <= row_n // NSA_HEADS) & (lane_n < n_new)
        bnew = bnear_ref[:, WINDOW:WINDOW + LANES][:, 0:_NEW_PAD]
        kn = newn_ref[0]
        s = jnp.where(own, _dot_t(nq, kn) + bnew, NEG_INF)
        _softmax_update(m, l, acc, s, kn)
        o_sel = acc[...] / l[...]
        _softmax_init(m, l, acc)
        for c in range(WINDOW // LANES):
            kw = win_ref[0, c * LANES:(c + 1) * LANES, :].astype(BF16)
            s = _dot_t(nq, kw) + bnear_ref[:, c * LANES:(c + 1) * LANES]
            if c == 0:
                s = jnp.where(lane >= row // NSA_HEADS, s, NEG_INF)
            _softmax_update(m, l, acc, s, kw)
        kwn = neww_ref[0]
        s = jnp.where(own, _dot_t(nq, kwn) + bnew, NEG_INF)
        _softmax_update(m, l, acc, s, kwn)
        o_win = acc[...] / l[...]
        g16 = _rows_from_tokens(misc_ref[0], r_n, NSA_HEADS)
        gate = lambda c: _pick_lane(g16, _L_GATE + c * NSA_HEADS + (row & (NSA_HEADS - 1)))
        onsa_ref[0] = gate(0) * ocmp[...] + gate(1) * o_sel + gate(2) * o_win


def _decode2(layer, page_table, nq, misc, cmp_eo, bkey, bnear, bcmp, win, neww, newn, cache_nsa, pp):
    b, n_pages = page_table.shape
    r_n = nq.shape[1]
    n_new = r_n // NSA_HEADS
    steps = n_pages // pp
    per_b = lambda a: pl.BlockSpec((1,) + a.shape[1:], lambda bi, jj, pt: (bi,) + (0,) * (a.ndim - 1))
    const = lambda a: pl.BlockSpec(a.shape, lambda bi, jj, pt: (0,) * a.ndim)
    assert n_pages * PAGE_SIZE // SEL_BLOCK <= LANES
    eblk = (jnp.arange(n_pages * PAGE_SIZE, dtype=I32)[None, :] // SEL_BLOCK
            == jnp.arange(LANES, dtype=I32)[:, None]).astype(BF16)
    in_specs = [per_b(nq), per_b(misc), per_b(cmp_eo),
                pl.BlockSpec((r_n, pp * LANES), lambda bi, jj, pt: (0, jj)), const(bnear), const(bcmp),
                per_b(win), per_b(neww), per_b(newn), const(eblk)]
    args = [nq, misc, cmp_eo, bkey, bnear, bcmp, win, neww, newn, eblk]
    for i in range(pp):
        in_specs.append(pl.BlockSpec((1, 1, LANES, PAGE_SIZE),
                                     lambda bi, jj, pt, i=i: (layer, pt[bi, jj * pp + i], 1, 0)))
        args.append(cache_nsa)
    grid_spec = pltpu.PrefetchScalarGridSpec(
        num_scalar_prefetch=1, grid=(b, steps), in_specs=in_specs,
        out_specs=pl.BlockSpec((1, r_n, LANES), lambda bi, jj, pt: (bi, 0, 0)),
        scratch_shapes=[pltpu.VMEM((steps, r_n, pp * LANES), F32), pltpu.VMEM((r_n, LANES), F32),
                        pltpu.VMEM((r_n, LANES), F32), pltpu.VMEM((r_n, LANES), F32), pltpu.VMEM((r_n, LANES), F32)])
    return pl.pallas_call(
        functools.partial(_decode2_kernel, pp=pp, n_new=n_new),
        grid_spec=grid_spec,
        out_shape=jax.ShapeDtypeStruct((b, r_n, LANES), F32),
        compiler_params=_cparams(("parallel", "arbitrary"), 40),
        name="decode_nsa",
    )(page_table, *args)


def _rope_swap(w):
    half = ROPE_DIM // 2
    return jnp.concatenate([w[..., half:], w[..., :half]], axis=-1)


def _pad_lanes(w, width=LANES, at=0):
    out = jnp.zeros(w.shape[:-1] + (width,), w.dtype)
    return out.at[..., at:at + w.shape[-1]].set(w)


def _prep_layer(l, w_in, b_f, g_q, g_kv, g_mix, w_uq, w_uk, w_uv, w_o):
    offs = np.concatenate([[0], np.cumsum(IN_SPLITS)])
    col = lambda i: w_in[l][:, offs[i]:offs[i + 1]]
    cq, ckv, kr, fq, fk, fv, ff, nq, kc, vc, ks, vs, kw, vw, ng = [col(i) for i in range(len(IN_SPLITS))]
    ga = _pad_lanes(jnp.concatenate([kr, ff, ng], axis=1))
    gb = _pad_lanes(_rope_swap(kr))
    fq4 = [_pad_lanes(fq[:, h * HEAD_DIM:(h + 1) * HEAD_DIM], at=(h // FOX_GROUP) * HEAD_DIM) for h in range(FOX_HEADS)]
    nq4 = [_pad_lanes(nq[:, h * HEAD_DIM:(h + 1) * HEAD_DIM]) for h in range(NSA_HEADS)]
    w1 = jnp.concatenate([cq, ckv, ga, gb] + fq4 + [fk, fv] + nq4 + [kc, vc, ks, vs, kw, vw], axis=1).astype(BF16)
    uq = w_uq[l]
    nope = uq[:, :, :NOPE_DIM].reshape(Q_RANK, MLA_HEADS * NOPE_DIM)
    rope = _pad_lanes(uq[:, :, NOPE_DIM:]).reshape(Q_RANK, MLA_HEADS * LANES)
    rsw = _pad_lanes(_rope_swap(uq[:, :, NOPE_DIM:])).reshape(Q_RANK, MLA_HEADS * LANES)
    wq = jnp.concatenate([nope, rope, rsw], axis=1).astype(BF16)
    wuk = jnp.zeros((MLA_HEADS * NOPE_DIM, MLA_HEADS * KV_RANK), F32)
    wuv = jnp.zeros((MLA_HEADS, KV_RANK, MLA_HEADS * MLA_V_DIM), F32)
    for h in range(MLA_HEADS):
        wuk = wuk.at[h * NOPE_DIM:(h + 1) * NOPE_DIM, h * KV_RANK:(h + 1) * KV_RANK].set(w_uk[l][:, h, :].T)
        wuv = wuv.at[h, :, h * MLA_V_DIM:(h + 1) * MLA_V_DIM].set(w_uv[l][:, h, :])
    wo = jnp.zeros((_MIX_PAD, D_MODEL), F32)
    n_mla = MLA_HEADS * MLA_V_DIM
    wo = wo.at[0:n_mla].set(w_o[l][0:n_mla])
    for h in range(FOX_HEADS):
        r0 = n_mla + h * LANES + (h // FOX_GROUP) * HEAD_DIM
        wo = wo.at[r0:r0 + HEAD_DIM].set(w_o[l][n_mla + h * HEAD_DIM:n_mla + (h + 1) * HEAD_DIM])
    for h in range(NSA_HEADS):
        r0 = n_mla + (FOX_HEADS + h) * LANES + HEAD_DIM
        src = n_mla + (FOX_HEADS + h) * HEAD_DIM
        wo = wo.at[r0:r0 + HEAD_DIM].set(w_o[l][src:src + HEAD_DIM])
    return dict(w1=w1, wq=wq, wuk=wuk.astype(BF16), wuv=wuv.astype(BF16), wo=wo.astype(BF16),
                g_mix=g_mix[l][None], g_q=g_q[l][None], g_kv=g_kv[l][None],
                bf=_pad_lanes(b_f[l][None], at=_L_LOGF))


def _rope_tables(pos):
    half = ROPE_DIM // 2
    inv = ROPE_BASE ** (-jnp.arange(half, dtype=F32) / half)
    ang = pos.astype(F32)[:, None] * inv
    cos, sin = jnp.cos(ang), jnp.sin(ang)
    return _pad_lanes(jnp.concatenate([cos, cos], axis=1)), _pad_lanes(jnp.concatenate([-sin, sin], axis=1))


_PROJ_NAMES = ("mla", "fkv", "nkv", "win", "misc", "qm", "kvm", "fq", "kvf", "nq", "kvn", "kvw", "cmp")


def _pad_rows(a, rows):
    return jnp.pad(a, ((0, 0), (0, rows - a.shape[1]), (0, 0)))


def kernel(x_prompt, x_sample, c_prompt, c_sample, cache_mla, cache_fox_kv, cache_fox_logf, cache_nsa_kv, state_nsa_win, page_table, w_ada, b_ada, g_mix, g_ffn, g_final, w_in, b_f, g_q, g_kv, w_uq, w_uk, w_uv, w_o, t5_table, w_pq, sub_keys, expert_u, expert_v):
    bp, sp, d = x_prompt.shape
    bs, ss, _ = x_sample.shape
    depth = w_in.shape[0]
    n_pool = cache_mla.shape[1]
    n_pages = page_table.shape[1]
    past_len = n_pages * PAGE_SIZE
    assert state_nsa_win.shape[2] == WINDOW and sp % QBLK == 0 and sp > QBLK
    np_, ns_ = bp * sp, bs * ss
    tm_p = 256
    tm_s = min(256, ns_)
    pp = next(p for p in (8, 4, 2, 1) if n_pages % p == 0)

    xp = x_prompt.reshape(np_, d)
    xs = x_sample.reshape(ns_, d)
    cache_mla = cache_mla.swapaxes(2, 3)
    cache_fox = cache_fox_kv.reshape(depth, n_pool, PAGE_SIZE, 2 * FOX_KV_HEADS * HEAD_DIM).swapaxes(2, 3)
    cache_nsa = cache_nsa_kv.reshape(depth, n_pool, PAGE_SIZE, 4 * HEAD_DIM).swapaxes(2, 3)
    logf_t = cache_fox_logf.swapaxes(2, 3)
    win_buf = state_nsa_win.reshape(depth, bs, WINDOW, 2 * HEAD_DIM)

    cm_p, sm_p = _rope_tables(jnp.arange(sp, dtype=I32))
    cm_s, sm_s = _rope_tables(past_len + jnp.arange(ss, dtype=I32))
    cm_s, sm_s = jnp.tile(cm_s, (tm_s // ss, 1)), jnp.tile(sm_s, (tm_s // ss, 1))
    toep, cmpb = _t5_prompt(t5_table, sp)
    bkey, bnear, bcmp = _t5_decode(t5_table, past_len, ss, WINDOW)
    rows_ih = lambda t: t[:, :ss].transpose(1, 0, 2).reshape(ss * NSA_HEADS, t.shape[2])
    bkey, bnear, bcmp = rows_ih(bkey), rows_ih(bnear), rows_ih(bcmp)

    c_all = jnp.concatenate([c_prompt, c_sample], axis=0)
    st_p, st_s = [], []
    for l in range(depth):
        lw = _prep_layer(l, w_in, b_f, g_q, g_kv, g_mix, w_uq, w_uk, w_uv, w_o)
        mod = _ada(c_all, w_ada[l], b_ada[l][None])
        mod_p = [m[:, None, :] for m in jnp.split(mod[:bp], 6, axis=-1)]
        mod_s = [jnp.repeat(m, ss, axis=0).reshape(ns_ // tm_s, tm_s, d) for m in jnp.split(mod[bp:], 6, axis=-1)]
        wpq = w_pq[l].astype(BF16)
        keys = sub_keys[l].reshape(PEER_HEADS * 2, N_KEYS, PEER_KEY_DIM).astype(BF16)
        tbl_u, tbl_v = _pack_table(expert_u[l]), _pack_table(expert_v[l])

        po = dict(zip(_PROJ_NAMES, _proj(xp, mod_p[1], mod_p[0], lw, cm_p, sm_p, tm_p)))
        cumc, cumr = _cum(po["misc"].reshape(bp, sp, LANES))
        mix = _attn_prompt(po, cumc, cumr, toep, cmpb, lw["wuv"], bp, sp)
        xp = _wo(mix, xp, mod_p[2], lw["wo"], tm_p)
        xp = _peer_block(xp, mod_p[4], mod_p[3], mod_p[5], g_ffn[l][None], wpq, keys, tbl_u, tbl_v, tm_p, 128)
        w_keep = min(WINDOW, sp)
        st_p.append((po["mla"].reshape(bp, sp, -1),
                     po["fkv"].reshape(bp, sp, 2, FOX_KV_HEADS, HEAD_DIM),
                     po["misc"][:, _L_LOGF:_L_GATE].reshape(bp, sp, FOX_HEADS),
                     po["nkv"].reshape(bp, sp, 4, HEAD_DIM),
                     po["win"].reshape(bp, sp, 2, HEAD_DIM)[:, sp - w_keep:]))

        so = dict(zip(_PROJ_NAMES, _proj(xs, mod_s[1], mod_s[0], lw, cm_s, sm_s, tm_s)))
        per_tok = lambda a: a.reshape(bs, ss, a.shape[1])
        newk = lambda a: _pad_rows(per_tok(a), _NEW_PAD)
        qm = so["qm"].reshape(bs, ss * MLA_HEADS, 2 * LANES)
        fq = so["fq"].reshape(bs, ss * FOX_HEADS, LANES)
        nq = so["nq"].reshape(bs, ss * NSA_HEADS, LANES)
        misc8 = _pad_rows(per_tok(so["misc"]), SUBLANES)
        omla, ofox, cmpm = _decode1(l, page_table, qm, fq, newk(so["kvm"]), newk(so["kvf"]), misc8, lw["wuv"],
                                    cache_mla, cache_fox, logf_t, cache_nsa, pp)
        cmpm = cmpm.reshape(bs, n_pages * (PAGE_SIZE // CMP_BLOCK), LANES)
        cmp_eo = jnp.concatenate([cmpm[:, 0::2], cmpm[:, 1::2]], axis=1)
        onsa = _decode2(l, page_table, nq, misc8, cmp_eo, bkey, bnear, bcmp, win_buf[l],
                        newk(so["kvw"]), newk(so["kvn"][:, 2 * HEAD_DIM:]), cache_nsa, pp)
        mix_s = jnp.concatenate([omla.reshape(ns_, -1), ofox.reshape(ns_, -1), onsa.reshape(ns_, -1)],
                                axis=1).astype(BF16)
        xs = _wo(mix_s, xs, mod_s[2], lw["wo"], tm_s)
        xs = _peer_block(xs, mod_s[4], mod_s[3], mod_s[5], g_ffn[l][None], wpq, keys, tbl_u, tbl_v, tm_s, 128)
        win_new = so["win"].reshape(bs, ss, 2, HEAD_DIM)
        win_all = jnp.concatenate([state_nsa_win[l], win_new], axis=1)
        st_s.append((per_tok(so["mla"]),
                     so["fkv"].reshape(bs, ss, 2, FOX_KV_HEADS, HEAD_DIM),
                     so["misc"][:, _L_LOGF:_L_GATE].reshape(bs, ss, FOX_HEADS),
                     so["nkv"].reshape(bs, ss, 4, HEAD_DIM),
                     win_all[:, win_all.shape[1] - min(WINDOW, win_all.shape[1]):]))

    yp = _final_norm(xp, g_final[None], tm_p).reshape(bp, sp, d)
    ys = _final_norm(xs, g_final[None], tm_s).reshape(bs, ss, d)
    stack = lambda st, i: jnp.stack([s[i] for s in st])
    return (yp, ys,
            stack(st_p, 0), stack(st_s, 0), stack(st_p, 1), stack(st_s, 1), stack(st_p, 2), stack(st_s, 2),
            stack(st_p, 3), stack(st_s, 3), stack(st_p, 4), stack(st_s, 4))
```

```python
import functools
import math

import numpy as np
import jax
import jax.numpy as jnp
from jax import lax
from jax.experimental import pallas as pl
from jax.experimental.pallas import tpu as pltpu

F32 = jnp.float32
BF16 = jnp.bfloat16
I32 = jnp.int32

D_MODEL = 1024
PAGE_SIZE = 128
HEAD_DIM = 64
MLA_HEADS = 8
Q_RANK = 256
KV_RANK = 128
ROPE_DIM = 32
NOPE_DIM = 64
MLA_V_DIM = 64
ROPE_BASE = 10000.0
FOX_HEADS = 4
FOX_KV_HEADS = 2
FOX_GROUP = FOX_HEADS // FOX_KV_HEADS
NSA_HEADS = 4
CMP_BLOCK = 32
SEL_BLOCK = 64
SEL_TOPK = 16
WINDOW = 512
FORCE_SCORE = 1000.0
N_BUCKETS = 32
MAX_DISTANCE = 1024
N_KEYS = 128
N_EXPERTS = N_KEYS * N_KEYS
PEER_HEADS = 8
PEER_TOPK = 16
PEER_KEY_DIM = 128
PEER_PICKS = PEER_HEADS * PEER_TOPK
QBLK = 128
EPS = 1e-6
NEG_INF = -1e30
POS_PAD = 2 ** 30
MLA_SCALE = (NOPE_DIM + ROPE_DIM) ** -0.5
HD_SCALE = HEAD_DIM ** -0.5
MIX_WIDTH = MLA_HEADS * MLA_V_DIM + FOX_HEADS * HEAD_DIM + NSA_HEADS * HEAD_DIM
IN_SPLITS = (Q_RANK, KV_RANK, ROPE_DIM,
             FOX_HEADS * HEAD_DIM, FOX_KV_HEADS * HEAD_DIM, FOX_KV_HEADS * HEAD_DIM, FOX_HEADS,
             NSA_HEADS * HEAD_DIM, HEAD_DIM, HEAD_DIM, HEAD_DIM, HEAD_DIM, HEAD_DIM, HEAD_DIM, 3 * NSA_HEADS)
IN_WIDTH = sum(IN_SPLITS)

SUBLANES = 8
LANES = 128
VMEM_BYTES_V7X = 64 * 1024 * 1024
HALF_EXPERTS = N_EXPERTS // 2


def _cparams(sem, vmem_mb=None):
    kw = dict(dimension_semantics=sem)
    if vmem_mb is not None:
        kw["vmem_limit_bytes"] = vmem_mb * 1024 * 1024
    return pltpu.CompilerParams(**kw)


def _topk_axis0(s, iota, k, payload=None):
    vals, outs = [], []
    for _ in range(k):
        m = jnp.max(s, axis=0, keepdims=True)
        first = jnp.min(jnp.where(s == m, iota, 1e9), axis=0, keepdims=True)
        onehot = iota == first
        if payload is None:
            outs.append(first)
        else:
            outs.append(jnp.max(jnp.where(onehot, payload, -1.0), axis=0, keepdims=True))
        s = jnp.where(onehot, -jnp.inf, s)
        vals.append(m)
    return vals, outs


_CAND_ROWS = ((0, PEER_TOPK),) + tuple((a, SUBLANES) for a in range(1, SUBLANES))
_N_CAND = PEER_TOPK + (SUBLANES - 1) * SUBLANES + SUBLANES


def _peer_route_kernel(x_ref, sc_ref, sh_ref, g_ref, wpq_ref, keys_ref,
                       h_ref, idx_ref, hi_ref, gate_ref):
    x = x_ref[...]
    y = x * lax.rsqrt(jnp.mean(x * x, axis=-1, keepdims=True) + EPS) * g_ref[...]
    h = y * (1.0 + sc_ref[0]) + sh_ref[0]
    h_ref[...] = h
    q = jnp.dot(h.astype(BF16), wpq_ref[...], preferred_element_type=F32).astype(BF16)
    tm = x.shape[0]
    iota_k = lax.broadcasted_iota(I32, (N_KEYS, tm), 0).astype(F32)
    rc = lax.broadcasted_iota(I32, (_N_CAND, tm), 0)
    mid = rc - PEER_TOPK
    flat = jnp.where(rc < PEER_TOPK, rc,
                     jnp.where(rc < _N_CAND - SUBLANES,
                               (1 + mid // SUBLANES) * PEER_TOPK + (mid & (SUBLANES - 1)),
                               (rc - (_N_CAND - 2 * SUBLANES)) * PEER_TOPK))
    iota_c = flat.astype(F32)
    for head in range(PEER_HEADS):
        tops = []
        for p in range(2):
            c = (head * 2 + p) * PEER_KEY_DIM
            s = lax.dot_general(keys_ref[head * 2 + p], q[:, c:c + PEER_KEY_DIM],
                                (((1,), (1,)), ((), ())), preferred_element_type=F32)
            tops.append(_topk_axis0(s, iota_k, PEER_TOPK))
        (v1, i1), (v2, i2) = tops
        v1s, i1s = jnp.concatenate(v1, axis=0), jnp.concatenate(i1, axis=0)
        v2s, i2s = jnp.concatenate(v2, axis=0), jnp.concatenate(i2, axis=0)
        cand = jnp.concatenate([v1[a] + v2s[0:nb] for a, nb in _CAND_ROWS] + [v1s[SUBLANES:] + v2[0]], axis=0)
        cidx = jnp.concatenate([i1[a] * float(N_KEYS) + i2s[0:nb] for a, nb in _CAND_ROWS]
                               + [i1s[SUBLANES:] * float(N_KEYS) + i2[0]], axis=0)
        tv, te = _topk_axis0(cand, iota_c, PEER_TOPK, payload=cidx)
        tv = jnp.concatenate(tv, axis=0)
        te = jnp.concatenate(te, axis=0).astype(I32)
        e = jnp.exp(tv - tv[0:1])
        gate = e / jnp.sum(e, axis=0, keepdims=True)
        rows = slice(head * PEER_TOPK, (head + 1) * PEER_TOPK)
        idx_ref[0, rows, :] = (te & (HALF_EXPERTS - 1)) * SUBLANES
        hi_ref[0, rows, :] = jnp.where(te >= HALF_EXPERTS, 1.0, 0.0)
        gate_ref[0, rows, :] = gate


def _peer_route(x, sc, sh, g, wpq, keys, tm):
    n = x.shape[0]
    nblk = n // tm
    bpg = nblk // sc.shape[0]
    mod_spec = pl.BlockSpec((1,) + sc.shape[1:], lambda i: (i // bpg, 0, 0))
    pick_spec = pl.BlockSpec((1, PEER_PICKS, tm), lambda i: (i, 0, 0))
    return pl.pallas_call(
        _peer_route_kernel,
        grid=(nblk,),
        in_specs=[pl.BlockSpec((tm, D_MODEL), lambda i: (i, 0)), mod_spec, mod_spec,
                  pl.BlockSpec((1, D_MODEL), lambda i: (0, 0)),
                  pl.BlockSpec(wpq.shape, lambda i: (0, 0)),
                  pl.BlockSpec(keys.shape, lambda i: (0, 0, 0))],
        out_specs=[pl.BlockSpec((tm, D_MODEL), lambda i: (i, 0)), pick_spec, pick_spec, pick_spec],
        out_shape=[jax.ShapeDtypeStruct((n, D_MODEL), F32),
                   jax.ShapeDtypeStruct((nblk, PEER_PICKS, tm), I32),
                   jax.ShapeDtypeStruct((nblk, PEER_PICKS, tm), F32),
                   jax.ShapeDtypeStruct((nblk, PEER_PICKS, tm), F32)],
        compiler_params=_cparams(("parallel",), 48),
        name="peer_route",
    )(x, sc, sh, g, wpq, keys)


_ROWS_PER_PICK = 2 * SUBLANES
PICK_ROWS = PEER_PICKS * _ROWS_PER_PICK


def _gelu_exact(x):
    return 0.5 * x * (1.0 + lax.erf(x * (2.0 ** -0.5)))


def _pick_expand():
    p = lax.broadcasted_iota(I32, (PEER_PICKS, PICK_ROWS), 0)
    k = lax.broadcasted_iota(I32, (PEER_PICKS, PICK_ROWS), 1)
    return jnp.where(k // _ROWS_PER_PICK == p, 1.0, 0.0).astype(BF16)


def _half_matches(hi_ref, expand):
    tm = hi_ref.shape[0]
    hi_cols = _dot(hi_ref[...].astype(BF16), expand)
    k = lax.broadcasted_iota(I32, (tm, PICK_ROWS), 1)
    return hi_cols == (k & 1).astype(F32)


def _chunk_diag(rows):
    s = lax.broadcasted_iota(I32, (rows, PICK_ROWS), 0)
    k = lax.broadcasted_iota(I32, (rows, PICK_ROWS), 1)
    return s == (k % _ROWS_PER_PICK) // 2


def _gather_rows(idx_ref, tbl_ref, g_scr, t):
    tok_idx = idx_ref.at[pl.ds(t * PEER_PICKS, PEER_PICKS)]
    for p in range(PEER_PICKS):
        r = pl.multiple_of(tok_idx[p], SUBLANES)
        g_scr[p * SUBLANES:(p + 1) * SUBLANES, :] = tbl_ref[pl.ds(r, SUBLANES), :]


_GATHER_AHEAD = 4
_GATHER_BUFS = 2 * _GATHER_AHEAD


def _staged_tokens(tm, gather, compute, bufs):
    n = len(bufs)
    for j in range(_GATHER_AHEAD):
        gather(bufs[j], j)

    def group(i, c):
        t0 = n * i
        for j in range(n):
            gather(bufs[(j + _GATHER_AHEAD) % n], jnp.minimum(t0 + j + _GATHER_AHEAD, tm - 1))
            compute(bufs[j], t0 + j)
        return c

    lax.fori_loop(0, tm // n, group, 0)


def _peer_act_kernel(idx_ref, hi_ref, h_ref, gate_ref, tbl_ref, w_ref, *scratch):
    bufs, rs_scr = scratch[:_GATHER_BUFS], scratch[_GATHER_BUFS]
    tm = h_ref.shape[0]
    diag = _chunk_diag(SUBLANES)

    def compute(g_scr, t):
        vt = pltpu.bitcast(g_scr[...], BF16)
        hb = h_ref[t].astype(BF16)
        r = _dot_t(jnp.concatenate([hb, hb], axis=0), vt)[0:SUBLANES]
        rs_scr[pl.ds(t, 1), :] = jnp.sum(jnp.where(diag, r, 0.0), axis=0, keepdims=True)

    _staged_tokens(tm, functools.partial(_gather_rows, idx_ref, tbl_ref), compute, bufs)
    expand = _pick_expand()
    x1, x2, x3 = _split3(jnp.where(_half_matches(hi_ref, expand), rs_scr[...], 0.0))
    act = (_dot_t(x1, expand) + _dot_t(x2, expand)) + _dot_t(x3, expand)
    w_ref[...] = gate_ref[...] * _gelu_exact(act)


def _peer_out_kernel(idx_ref, hi_ref, w_ref, x_ref, gt_ref, tbl_ref, o_ref, *scratch):
    bufs, ws_scr = scratch[:_GATHER_BUFS], scratch[_GATHER_BUFS]
    tm = x_ref.shape[0]
    expand = _pick_expand()
    w_cols = _dot(w_ref[...].astype(BF16), expand)
    ws_scr[...] = jnp.where(_half_matches(hi_ref, expand), w_cols, 0.0)
    diag = _chunk_diag(_ROWS_PER_PICK)

    def compute(g_scr, t):
        vt = pltpu.bitcast(g_scr[...], BF16)
        wrow = jnp.broadcast_to(ws_scr[pl.ds(t, 1), :], (_ROWS_PER_PICK, PICK_ROWS))
        o_ref[t] = _dot(jnp.where(diag, wrow, 0.0).astype(BF16), vt)[0:SUBLANES]

    _staged_tokens(tm, functools.partial(_gather_rows, idx_ref, tbl_ref), compute, bufs)
    o_ref[...] = x_ref[...] + gt_ref[0] * o_ref[...]


def _idx_spec(tm):
    return pl.BlockSpec((tm * PEER_PICKS,), lambda i: (i,), memory_space=pltpu.SMEM)


def _table_spec():
    return pl.BlockSpec((HALF_EXPERTS * SUBLANES, LANES), lambda i: (0, 0), pipeline_mode=pl.Buffered(1))


def _gather_scratch():
    return [pltpu.VMEM((PEER_PICKS * SUBLANES, LANES), I32) for _ in range(_GATHER_BUFS)]


def _peer_act(idx, hi, h3, gate, tbl, tm):
    n = h3.shape[0]
    picks = pl.BlockSpec((tm, PEER_PICKS), lambda i: (i, 0))
    return pl.pallas_call(
        _peer_act_kernel,
        grid=(n // tm,),
        in_specs=[_idx_spec(tm), picks, pl.BlockSpec((tm, SUBLANES, LANES), lambda i: (i, 0, 0)), picks,
                  _table_spec()],
        out_specs=picks,
        out_shape=jax.ShapeDtypeStruct((n, PEER_PICKS), F32),
        scratch_shapes=_gather_scratch() + [pltpu.VMEM((tm, PICK_ROWS), F32)],
        compiler_params=_cparams(("parallel",), 52),
        name="peer_act",
    )(idx, hi, h3, gate, tbl)


def _peer_out(idx, hi, w, x3, gt3, tbl, tm):
    n = x3.shape[0]
    bpg = (n // tm) // gt3.shape[0]
    tok = pl.BlockSpec((tm, SUBLANES, LANES), lambda i: (i, 0, 0))
    picks = pl.BlockSpec((tm, PEER_PICKS), lambda i: (i, 0))
    gt_spec = pl.BlockSpec((1,) + gt3.shape[1:], lambda i: (i // bpg, 0, 0, 0))
    return pl.pallas_call(
        _peer_out_kernel,
        grid=(n // tm,),
        in_specs=[_idx_spec(tm), picks, picks, tok, gt_spec, _table_spec()],
        out_specs=tok,
        out_shape=jax.ShapeDtypeStruct((n, SUBLANES, LANES), F32),
        scratch_shapes=_gather_scratch() + [pltpu.VMEM((tm, PICK_ROWS), F32)],
        compiler_params=_cparams(("parallel",), 52),
        name="peer_out",
    )(idx, hi, w, x3, gt3, tbl)


def _pack_table(t):
    b = lax.bitcast_convert_type(t.astype(BF16), jnp.uint16).astype(jnp.uint32)
    packed = b[:HALF_EXPERTS] | (b[HALF_EXPERTS:] << 16)
    return lax.bitcast_convert_type(packed, I32).reshape(HALF_EXPERTS * SUBLANES, LANES)


def _picks_token_major(a):
    nblk, p, tm = a.shape
    return a.transpose(0, 2, 1).reshape(nblk * tm, p)


def _peer_block(x, sc2, sh2, gt2, g_f, wpq_bf, keys_bf, tbl_u, tbl_v, tm_route, tm_pass):
    n = x.shape[0]
    h, idx, hi, gate = _peer_route(x, sc2, sh2, g_f, wpq_bf, keys_bf, tm_route)
    idx = _picks_token_major(idx).reshape(n * PEER_PICKS)
    hi, gate = _picks_token_major(hi), _picks_token_major(gate)
    w = _peer_act(idx, hi, h.reshape(n, SUBLANES, LANES), gate, tbl_u, tm_pass)
    if gt2.shape[1] == 1:
        gt3 = gt2.reshape(gt2.shape[0], 1, SUBLANES, LANES)
    else:
        gt3 = gt2.reshape(n // tm_pass, tm_pass, SUBLANES, LANES)
    out = _peer_out(idx, hi, w, x.reshape(n, SUBLANES, LANES), gt3, tbl_v, tm_pass)
    return out.reshape(n, D_MODEL)


def _rms(x, g):
    return x * lax.rsqrt(jnp.mean(x * x, axis=-1, keepdims=True) + EPS) * g


def _dot(a, b):
    return jnp.dot(a, b, preferred_element_type=F32)


def _dot_t(a, b):
    return lax.dot_general(a, b, (((1,), (1,)), ((), ())), preferred_element_type=F32)


def _sigmoid(x):
    return 1.0 / (1.0 + jnp.exp(-x))


def _softmax_update(m_ref, l_ref, acc_ref, s, v):
    m_old = m_ref[...]
    m_new = jnp.maximum(m_old, jnp.max(s, axis=1, keepdims=True))
    a = jnp.exp(m_old - m_new)
    p = jnp.exp(s - m_new)
    l_ref[...] = a * l_ref[...] + jnp.sum(p, axis=1, keepdims=True)
    acc_ref[...] = a * acc_ref[...] + _dot(p.astype(BF16), v)
    m_ref[...] = m_new


def _softmax_init(m_ref, l_ref, acc_ref):
    m_ref[...] = jnp.full(m_ref.shape, NEG_INF, F32)
    l_ref[...] = jnp.zeros(l_ref.shape, F32)
    acc_ref[...] = jnp.zeros(acc_ref.shape, F32)


def _topk_mask_lanes(imp, lane, k):
    sel = jnp.zeros(imp.shape, F32)
    for _ in range(k):
        m = jnp.max(imp, axis=1, keepdims=True)
        first = jnp.min(jnp.where(imp == m, lane, 1e9), axis=1, keepdims=True)
        hit = lane == first
        sel = jnp.where(hit, 1.0, sel)
        imp = jnp.where(hit, -jnp.inf, imp)
    return sel


def _ada_kernel(c_ref, w_ref, b_ref, o_ref):
    c = c_ref[...]
    o_ref[...] = jnp.dot(c * _sigmoid(c), w_ref[...], preferred_element_type=F32,
                         precision=lax.Precision.HIGHEST) + b_ref[...]


def _ada(c, w, b):
    nb, d = c.shape
    n_out = w.shape[1]
    tn = 1536
    return pl.pallas_call(
        _ada_kernel,
        grid=(n_out // tn,),
        in_specs=[pl.BlockSpec((nb, d), lambda j: (0, 0)),
                  pl.BlockSpec((d, tn), lambda j: (0, j)),
                  pl.BlockSpec((1, tn), lambda j: (0, j))],
        out_specs=pl.BlockSpec((nb, tn), lambda j: (0, j)),
        out_shape=jax.ShapeDtypeStruct((nb, n_out), F32),
        compiler_params=_cparams(("parallel",), 40),
        name="ada_mod",
    )(c, w, b)


_C_CQ, _C_CKV, _C_GA, _C_GB, _C_FQ, _C_FKV, _C_NQ, _C_NKV, _C_WKV, _C_END = (
    0, 256, 384, 512, 640, 1152, 1408, 1920, 2176, 2304)
_L_LOGF = ROPE_DIM
_L_GATE = ROPE_DIM + FOX_HEADS
_L_GEND = _L_GATE + 3 * NSA_HEADS
_Q_NOPE, _Q_ROPE, _Q_RSW, _Q_END = 0, 512, 1536, 2560


def _proj_kernel(x_ref, sc_ref, sh_ref, gm_ref, w1_ref, gq_ref, gkv_ref, wq_ref, wuk_ref, bf_ref, cm_ref, sm_ref,
                 mla_ref, fkv_ref, nkv_ref, win_ref, misc_ref,
                 qm_ref, kvm_ref, fq_ref, kvf_ref, nq_ref, kvn_ref, kvw_ref, cmp_ref):
    tm = x_ref.shape[0]
    h = _rms(x_ref[...], gm_ref[...]) * (1.0 + sc_ref[0]) + sh_ref[0]
    proj = _dot(h.astype(BF16), w1_ref[...])
    cm = cm_ref[...]
    sm = sm_ref[...]
    cqn = _rms(proj[:, _C_CQ:_C_CKV], gq_ref[...])
    q2 = _dot(cqn.astype(BF16), wq_ref[...])
    qlat = _dot(q2[:, _Q_NOPE:_Q_ROPE].astype(BF16), wuk_ref[...])
    for hd in range(MLA_HEADS):
        lo = hd * LANES
        rot = q2[:, _Q_ROPE + lo:_Q_ROPE + lo + LANES] * cm + q2[:, _Q_RSW + lo:_Q_RSW + lo + LANES] * sm
        qm_ref[:, 2 * lo:2 * lo + LANES] = (qlat[:, lo:lo + LANES] * MLA_SCALE).astype(BF16)
        qm_ref[:, 2 * lo + LANES:2 * lo + 2 * LANES] = (rot * MLA_SCALE).astype(BF16)
    ckvn = _rms(proj[:, _C_CKV:_C_GA], gkv_ref[...])
    ga = proj[:, _C_GA:_C_GB]
    krot = ga * cm + proj[:, _C_GB:_C_FQ] * sm
    mla_ref[:, 0:KV_RANK] = ckvn
    mla_ref[:, KV_RANK:KV_RANK + ROPE_DIM] = krot[:, 0:ROPE_DIM]
    kvm_ref[:, 0:LANES] = ckvn.astype(BF16)
    kvm_ref[:, LANES:2 * LANES] = krot.astype(BF16)
    lane = lax.broadcasted_iota(I32, (tm, LANES), 1)
    z = ga + bf_ref[...]
    logsig = jnp.minimum(z, 0.0) - jnp.log(1.0 + jnp.exp(-jnp.abs(z)))
    misc_ref[...] = jnp.where(lane < _L_LOGF, krot,
                              jnp.where(lane < _L_GATE, logsig,
                                        jnp.where(lane < _L_GEND, _sigmoid(ga), 0.0)))
    fq_ref[...] = (proj[:, _C_FQ:_C_FKV] * HD_SCALE).astype(BF16)
    nq_ref[...] = (proj[:, _C_NQ:_C_NKV] * HD_SCALE).astype(BF16)
    fkv = proj[:, _C_FKV:_C_NQ]
    nkv = proj[:, _C_NKV:_C_WKV]
    wkv = proj[:, _C_WKV:_C_END]
    fkv_ref[...] = fkv
    nkv_ref[...] = nkv
    win_ref[...] = wkv
    kvf_ref[...] = fkv.astype(BF16)
    kvn_ref[...] = nkv.astype(BF16)
    kvw_ref[...] = wkv.astype(BF16)
    cmp_ref[...] = jnp.sum(nkv[:, 0:LANES].reshape(tm // CMP_BLOCK, CMP_BLOCK, LANES), axis=1) * (1.0 / CMP_BLOCK)


def _proj(x, sc, sh, lw, cm, sm, tm):
    n = x.shape[0]
    nblk = n // tm
    bpg = nblk // sc.shape[0]
    pblk = cm.shape[0] // tm
    mod_spec = pl.BlockSpec((1,) + sc.shape[1:], lambda i: (i // bpg, 0, 0))
    const = lambda a: pl.BlockSpec(a.shape, lambda i: (0,) * a.ndim)
    tok = lambda w: pl.BlockSpec((tm, w), lambda i: (i, 0))
    rot_spec = pl.BlockSpec((tm, LANES), lambda i: (i % pblk, 0))
    widths = [(KV_RANK + ROPE_DIM, F32), (2 * LANES, F32), (2 * LANES, F32), (LANES, F32), (LANES, F32),
              (MLA_HEADS * 2 * LANES, BF16), (2 * LANES, BF16), (FOX_HEADS * LANES, BF16), (2 * LANES, BF16),
              (NSA_HEADS * LANES, BF16), (2 * LANES, BF16), (LANES, BF16)]
    out_specs = [tok(w) for w, _ in widths] + [pl.BlockSpec((tm // CMP_BLOCK, LANES), lambda i: (i, 0))]
    out_shape = [jax.ShapeDtypeStruct((n, w), dt) for w, dt in widths] + [
        jax.ShapeDtypeStruct((n // CMP_BLOCK, LANES), F32)]
    return pl.pallas_call(
        _proj_kernel,
        grid=(nblk,),
        in_specs=[tok(D_MODEL), mod_spec, mod_spec, const(lw["g_mix"]), const(lw["w1"]), const(lw["g_q"]),
                  const(lw["g_kv"]), const(lw["wq"]), const(lw["wuk"]), const(lw["bf"]), rot_spec, rot_spec],
        out_specs=out_specs,
        out_shape=out_shape,
        compiler_params=_cparams(("parallel",), 48),
        name="in_proj",
    )(x, sc, sh, lw["g_mix"], lw["w1"], lw["g_q"], lw["g_kv"], lw["wq"], lw["wuk"], lw["bf"], cm, sm)


def _split3(x):
    x1 = x.astype(BF16)
    r = x - x1.astype(F32)
    x2 = r.astype(BF16)
    x3 = (r - x2.astype(F32)).astype(BF16)
    return x1, x2, x3


def _cum_kernel(misc_ref, col_ref, row_ref):
    nblk = misc_ref.shape[1] // LANES
    ri = lax.broadcasted_iota(I32, (LANES, LANES), 0)
    ci = lax.broadcasted_iota(I32, (LANES, LANES), 1)
    tri = jnp.where(ci <= ri, 1.0, 0.0).astype(BF16)
    carry = jnp.zeros((1, LANES), F32)
    for blk in range(nblk):
        x1, x2, x3 = _split3(misc_ref[0, blk * LANES:(blk + 1) * LANES, :])
        c = (_dot(tri, x1) + _dot(tri, x2)) + _dot(tri, x3) + carry
        carry = c[LANES - 1:LANES, :]
        col_ref[0, blk * LANES:(blk + 1) * LANES, :] = c
        row_ref[0, blk] = c.T[_L_LOGF:_L_LOGF + SUBLANES, :]


def _cum(misc3):
    b, s, _ = misc3.shape
    return pl.pallas_call(
        _cum_kernel,
        grid=(b,),
        in_specs=[pl.BlockSpec((1, s, LANES), lambda i: (i, 0, 0))],
        out_specs=[pl.BlockSpec((1, s, LANES), lambda i: (i, 0, 0)),
                   pl.BlockSpec((1, s // LANES, SUBLANES, LANES), lambda i: (i, 0, 0, 0))],
        out_shape=[jax.ShapeDtypeStruct((b, s, LANES), F32),
                   jax.ShapeDtypeStruct((b, s // LANES, SUBLANES, LANES), F32)],
        compiler_params=_cparams(("parallel",)),
        name="fox_cumsum",
    )(misc3)


_T5_EXACT = N_BUCKETS // 2
_T5_THRESH = tuple(int(math.ceil(_T5_EXACT * (MAX_DISTANCE / _T5_EXACT) ** (j / (N_BUCKETS - _T5_EXACT)) - 1e-9))
                   for j in range(1, N_BUCKETS - _T5_EXACT))


def _t5_bias(tbl_ref, dist):
    n = jnp.maximum(dist, 0)
    big = jnp.full(n.shape, _T5_EXACT, I32)
    for t in _T5_THRESH:
        big = big + jnp.where(n >= t, 1, 0)
    bucket = jnp.where(n < _T5_EXACT, n, big)
    outs = [jnp.zeros(n.shape, F32) for _ in range(NSA_HEADS)]
    for j in range(N_BUCKETS):
        hit = bucket == j
        for h in range(NSA_HEADS):
            outs[h] = jnp.where(hit, tbl_ref[j, h], outs[h])
    return outs


def _cmp_block_of_lane(lane, half):
    return jnp.where(lane < half, 2 * lane, 2 * (lane - half) + 1)


def _t5_prompt_kernel(tbl_ref, toep_ref, cmpb_ref):
    nq = toep_ref.shape[1]
    ncmp = cmpb_ref.shape[3]
    i = lax.broadcasted_iota(I32, (QBLK, QBLK), 0)
    j = lax.broadcasted_iota(I32, (QBLK, QBLK), 1)
    ic = lax.broadcasted_iota(I32, (QBLK, ncmp), 0)
    lc = lax.broadcasted_iota(I32, (QBLK, ncmp), 1)
    cmp_end = _cmp_block_of_lane(lc, ncmp // 2) * CMP_BLOCK + (CMP_BLOCK - 1)
    for off in range(nq):
        for h, v in enumerate(_t5_bias(tbl_ref, off * QBLK + i - j)):
            toep_ref[h, off] = v
        for h, v in enumerate(_t5_bias(tbl_ref, off * QBLK + ic - cmp_end)):
            cmpb_ref[h, off] = v


def _t5_prompt(t5_table, s):
    nq = s // QBLK
    ncmp = s // CMP_BLOCK
    return pl.pallas_call(
        _t5_prompt_kernel,
        in_specs=[pl.BlockSpec(memory_space=pltpu.SMEM)],
        out_shape=[jax.ShapeDtypeStruct((NSA_HEADS, nq, QBLK, QBLK), F32),
                   jax.ShapeDtypeStruct((NSA_HEADS, nq, QBLK, ncmp), F32)],
        compiler_params=pltpu.CompilerParams(vmem_limit_bytes=40 * 1024 * 1024),
        name="t5_prompt_tables",
    )(t5_table)


def _t5_decode_kernel(tbl_ref, key_ref, near_ref, cmpb_ref, *, past_len, n_new):
    lp = key_ref.shape[2]
    ncmp = cmpb_ref.shape[2]
    wl = near_ref.shape[2]
    qi = lambda w: lax.broadcasted_iota(I32, (SUBLANES, w), 0)
    ln = lambda w: lax.broadcasted_iota(I32, (SUBLANES, w), 1)
    for h, v in enumerate(_t5_bias(tbl_ref, past_len + qi(lp) - ln(lp))):
        key_ref[h] = v
    for h, v in enumerate(_t5_bias(tbl_ref, WINDOW + qi(wl) - ln(wl))):
        near_ref[h] = v
    cmp_end = _cmp_block_of_lane(ln(ncmp), ncmp // 2) * CMP_BLOCK + (CMP_BLOCK - 1)
    for h, v in enumerate(_t5_bias(tbl_ref, past_len + qi(ncmp) - cmp_end)):
        cmpb_ref[h] = v


def _t5_decode(t5_table, past_len, n_new, w_buf):
    del w_buf
    return pl.pallas_call(
        functools.partial(_t5_decode_kernel, past_len=past_len, n_new=n_new),
        in_specs=[pl.BlockSpec(memory_space=pltpu.SMEM)],
        out_shape=[jax.ShapeDtypeStruct((NSA_HEADS, SUBLANES, past_len), F32),
                   jax.ShapeDtypeStruct((NSA_HEADS, SUBLANES, WINDOW + LANES), F32),
                   jax.ShapeDtypeStruct((NSA_HEADS, SUBLANES, past_len // CMP_BLOCK), F32)],
        compiler_params=pltpu.CompilerParams(vmem_limit_bytes=40 * 1024 * 1024),
        name="t5_decode_tables",
    )(t5_table)


def _attn_prompt_kernel(qm_ref, fq_ref, nq_ref, cumc_ref, misc_ref, cmpb_ref,
                        kvm_ref, kvf_ref, kvn_ref, kvw_ref, cmp_ref, cumr_ref, toep_ref, wuv_ref, eblk_ref,
                        mix_ref,
                        q8_scr, fq_scr, nq_scr, cq_scr, selk_scr, m_scr, l_scr, acc_scr,
                        fm_scr, fl_scr, fa_scr, sm_scr, sl_scr, sa_scr):
    qi = pl.program_id(1)
    s_len = kvm_ref.shape[1]
    n_kb = s_len // QBLK
    n_cmp = s_len // CMP_BLOCK
    n_sel = s_len // SEL_BLOCK
    top = min(SEL_TOPK, n_sel)
    q0 = qi * QBLK

    def rows_pos(nh):
        r = lax.broadcasted_iota(I32, (nh * QBLK, QBLK), 0)
        return q0 + (r & (QBLK - 1))

    def key_pos(nh, kb):
        return kb * QBLK + lax.broadcasted_iota(I32, (nh * QBLK, QBLK), 1)

    for h in range(MLA_HEADS):
        q8_scr[h * QBLK:(h + 1) * QBLK, :] = qm_ref[:, h * 2 * LANES:(h + 1) * 2 * LANES]
    cumc = cumc_ref[0]
    for h in range(FOX_HEADS):
        fq_scr[h * QBLK:(h + 1) * QBLK, :] = fq_ref[:, h * LANES:(h + 1) * LANES]
        cq_scr[h * QBLK:(h + 1) * QBLK, :] = jnp.broadcast_to(cumc[:, _L_LOGF + h:_L_LOGF + h + 1], (QBLK, LANES))
    for h in range(NSA_HEADS):
        nq_scr[h * QBLK:(h + 1) * QBLK, :] = nq_ref[:, h * LANES:(h + 1) * LANES]

    _softmax_init(m_scr, l_scr, acc_scr)
    qp8 = rows_pos(MLA_HEADS)

    def mla_step(kb, diagonal):
        k = kvm_ref[0, pl.ds(pl.multiple_of(kb * QBLK, QBLK), QBLK), :]
        s = _dot_t(q8_scr[...], k)
        if diagonal:
            s = jnp.where(key_pos(MLA_HEADS, kb) <= qp8, s, NEG_INF)
        _softmax_update(m_scr, l_scr, acc_scr, s, k[:, 0:LANES])

    nr = FOX_HEADS * QBLK
    _softmax_init(fm_scr, fl_scr, fa_scr)
    qp4 = rows_pos(FOX_HEADS)

    def fox_step(kb, diagonal):
        k = kvf_ref[0, pl.ds(pl.multiple_of(kb * QBLK, QBLK), QBLK), :]
        ck = cumr_ref[0, kb]
        ck4 = jnp.concatenate([jnp.broadcast_to(ck[h:h + 1, :], (QBLK, LANES)) for h in range(FOX_HEADS)], axis=0)
        s = _dot_t(fq_scr[...], k[:, 0:LANES]) + (cq_scr[...] - ck4)
        if diagonal:
            s = jnp.where(key_pos(FOX_HEADS, kb) <= qp4, s, NEG_INF)
        _softmax_update(fm_scr, fl_scr, fa_scr, s, k[:, LANES:2 * LANES])

    half = n_cmp // 2
    cmpk = jnp.concatenate([cmp_ref[0, pl.ds(0, half, stride=2), :], cmp_ref[0, pl.ds(1, half, stride=2), :]],
                           axis=0).astype(BF16)
    nqv = nq_scr[...]
    bias_c = jnp.concatenate([cmpb_ref[h, 0] for h in range(NSA_HEADS)], axis=0)
    lane_c = lax.broadcasted_iota(I32, (nr, n_cmp), 1)
    cmp_end = _cmp_block_of_lane(lane_c, half) * CMP_BLOCK + (CMP_BLOCK - 1)
    qpc = q0 + (lax.broadcasted_iota(I32, (nr, n_cmp), 0) & (QBLK - 1))
    valid_c = cmp_end <= qpc
    s = jnp.where(valid_c, _dot_t(nqv, cmpk) + bias_c, NEG_INF)
    p = jnp.exp(s - jnp.max(s, axis=1, keepdims=True))
    pc = jnp.where(valid_c, p / jnp.sum(p, axis=1, keepdims=True), 0.0)
    o_cmp = _dot(pc.astype(BF16), cmpk)

    pcs = pc[0:QBLK]
    for h in range(1, NSA_HEADS):
        pcs = pcs + pc[h * QBLK:(h + 1) * QBLK]
    imp = pcs[:, 0:half] + pcs[:, half:n_cmp]
    imp_t = jnp.concatenate([imp, jnp.zeros((QBLK, LANES - n_sel), F32)], axis=1).T[0:n_sel, :]
    blk = lax.broadcasted_iota(I32, (n_sel, QBLK), 0)
    qps = q0 + lax.broadcasted_iota(I32, (n_sel, QBLK), 1)
    forced = (blk == qps // SEL_BLOCK) | (blk == 0)
    imp_t = jnp.where(forced, FORCE_SCORE, imp_t)
    imp_t = jnp.where(blk * SEL_BLOCK > qps, -1.0, imp_t)
    rank = jnp.zeros((n_sel, QBLK), F32)
    for i in range(n_sel):
        row = imp_t[i:i + 1, :]
        rank = rank + jnp.where((row > imp_t) | ((row == imp_t) & (blk > i)), 1.0, 0.0)
    sel_t = jnp.where(rank < top, 1.0, 0.0)
    sel = jnp.concatenate([sel_t, jnp.zeros((LANES - n_sel, QBLK), F32)], axis=0).T
    selk = _dot(sel.astype(BF16), eblk_ref[...])
    for kb in range(n_kb):
        selk_scr[kb] = selk[:, kb * QBLK:(kb + 1) * QBLK]

    def toep4(off):
        return jnp.concatenate([toep_ref[h, off] for h in range(NSA_HEADS)], axis=0)

    _softmax_init(sm_scr, sl_scr, sa_scr)

    def sel_step(kb, diagonal):
        k = kvn_ref[0, pl.ds(pl.multiple_of(kb * QBLK, QBLK), QBLK), :][:, LANES:2 * LANES]
        s = _dot_t(nq_scr[...], k) + toep4(qi - kb)
        mk = selk_scr[kb]
        keep = jnp.concatenate([mk] * NSA_HEADS, axis=0) > 0.5
        if diagonal:
            keep = keep & (key_pos(NSA_HEADS, kb) <= qp4)
        _softmax_update(sm_scr, sl_scr, sa_scr, jnp.where(keep, s, NEG_INF), k)

    def causal_body(kb, c):
        mla_step(kb, False)
        fox_step(kb, False)
        sel_step(kb, False)
        return c

    lax.fori_loop(0, qi, causal_body, 0)
    mla_step(qi, True)
    fox_step(qi, True)
    sel_step(qi, True)

    o_lat = (acc_scr[...] / l_scr[...]).astype(BF16)
    o_mla = _dot(o_lat[0:QBLK], wuv_ref[0])
    for h in range(1, MLA_HEADS):
        o_mla = o_mla + _dot(o_lat[h * QBLK:(h + 1) * QBLK], wuv_ref[h])
    mix_ref[:, 0:MLA_HEADS * MLA_V_DIM] = o_mla.astype(BF16)
    o_fox = fa_scr[...] / fl_scr[...]
    base = MLA_HEADS * MLA_V_DIM
    for h in range(FOX_HEADS):
        mix_ref[:, base + h * LANES:base + (h + 1) * LANES] = o_fox[h * QBLK:(h + 1) * QBLK].astype(BF16)
    o_sel = sa_scr[...] / sl_scr[...]

    m_r, l_r, a_r = fm_scr, fl_scr, fa_scr
    _softmax_init(m_r, l_r, a_r)

    def win_step(kb, c):
        k = kvw_ref[0, pl.ds(pl.multiple_of(kb * QBLK, QBLK), QBLK), :]
        s = _dot_t(nq_scr[...], k) + toep4(qi - kb)
        dist = qp4 - key_pos(NSA_HEADS, kb)
        s = jnp.where((dist >= 0) & (dist <= WINDOW), s, NEG_INF)
        _softmax_update(m_r, l_r, a_r, s, k)
        return c

    lax.fori_loop(jnp.maximum(qi - WINDOW // QBLK, 0), qi + 1, win_step, 0)
    o_win = a_r[...] / l_r[...]

    misc = misc_ref[...]
    base = MLA_HEADS * MLA_V_DIM + FOX_HEADS * LANES
    for h in range(NSA_HEADS):
        rows = slice(h * QBLK, (h + 1) * QBLK)
        gate = lambda c: misc[:, _L_GATE + c * NSA_HEADS + h:_L_GATE + c * NSA_HEADS + h + 1]
        o = gate(0) * o_cmp[rows] + gate(1) * o_sel[rows] + gate(2) * o_win[rows]
        mix_ref[:, base + h * LANES:base + (h + 1) * LANES] = o.astype(BF16)


_MIX_PAD = MLA_HEADS * MLA_V_DIM + (FOX_HEADS + NSA_HEADS) * LANES


def _attn_prompt(po, cumc, cumr, toep, cmpb, wuv, b, s):
    nq = s // QBLK
    qblk = lambda w: pl.BlockSpec((QBLK, w), lambda bi, qi: (bi * nq + qi, 0))
    per_b = lambda a: pl.BlockSpec((1,) + a.shape[1:], lambda bi, qi: (bi,) + (0,) * (a.ndim - 1))
    const = lambda a: pl.BlockSpec(a.shape, lambda bi, qi: (0,) * a.ndim)
    kvm = po["kvm"].reshape(b, s, 2 * LANES)
    kvf = po["kvf"].reshape(b, s, 2 * LANES)
    kvn = po["kvn"].reshape(b, s, 2 * LANES)
    kvw = po["kvw"].reshape(b, s, LANES)
    cmpm = po["cmp"].reshape(b, s // CMP_BLOCK, LANES)
    rows = MLA_HEADS * QBLK
    assert s // SEL_BLOCK <= LANES
    eblk = (jnp.arange(s, dtype=I32)[None, :] // SEL_BLOCK == jnp.arange(LANES, dtype=I32)[:, None]).astype(BF16)
    return pl.pallas_call(
        _attn_prompt_kernel,
        grid=(b, nq),
        in_specs=[qblk(MLA_HEADS * 2 * LANES), qblk(FOX_HEADS * LANES), qblk(NSA_HEADS * LANES),
                  pl.BlockSpec((1, QBLK, LANES), lambda bi, qi: (bi, qi, 0)), qblk(LANES),
                  pl.BlockSpec((NSA_HEADS, 1, QBLK, s // CMP_BLOCK), lambda bi, qi: (0, qi, 0, 0)),
                  per_b(kvm), per_b(kvf), per_b(kvn), per_b(kvw), per_b(cmpm), per_b(cumr),
                  const(toep), const(wuv), const(eblk)],
        out_specs=qblk(_MIX_PAD),
        out_shape=jax.ShapeDtypeStruct((b * s, _MIX_PAD), BF16),
        scratch_shapes=[pltpu.VMEM((rows, 2 * LANES), BF16),
                        pltpu.VMEM((FOX_HEADS * QBLK, LANES), BF16),
                        pltpu.VMEM((NSA_HEADS * QBLK, LANES), BF16),
                        pltpu.VMEM((FOX_HEADS * QBLK, LANES), F32),
                        pltpu.VMEM((nq, QBLK, QBLK), F32),
                        pltpu.VMEM((rows, LANES), F32), pltpu.VMEM((rows, LANES), F32),
                        pltpu.VMEM((rows, LANES), F32)]
                       + [pltpu.VMEM((FOX_HEADS * QBLK, LANES), F32) for _ in range(6)],
        compiler_params=_cparams(("parallel", "arbitrary"), 48),
        name="attn_prompt",
    )(po["qm"], po["fq"], po["nq"], cumc, po["misc"], cmpb, kvm, kvf, kvn, kvw, cmpm, cumr, toep, wuv, eblk)


def _wo_kernel(mix_ref, x_ref, gt_ref, w_ref, o_ref):
    o_ref[...] = x_ref[...] + gt_ref[0] * _dot(mix_ref[...], w_ref[...])


def _wo(mix, x, gt, w, tm):
    n = x.shape[0]
    nblk = n // tm
    bpg = nblk // gt.shape[0]
    return pl.pallas_call(
        _wo_kernel,
        grid=(nblk,),
        in_specs=[pl.BlockSpec((tm, mix.shape[1]), lambda i: (i, 0)),
                  pl.BlockSpec((tm, D_MODEL), lambda i: (i, 0)),
                  pl.BlockSpec((1,) + gt.shape[1:], lambda i: (i // bpg, 0, 0)),
                  pl.BlockSpec(w.shape, lambda i: (0, 0))],
        out_specs=pl.BlockSpec((tm, D_MODEL), lambda i: (i, 0)),
        out_shape=jax.ShapeDtypeStruct((n, D_MODEL), F32),
        compiler_params=_cparams(("parallel",), 40),
        name="out_proj",
    )(mix, x, gt, w)


def _final_norm_kernel(x_ref, g_ref, o_ref):
    o_ref[...] = _rms(x_ref[...], g_ref[...])


def _final_norm(x, g, tm):
    n = x.shape[0]
    return pl.pallas_call(
        _final_norm_kernel,
        grid=(n // tm,),
        in_specs=[pl.BlockSpec((tm, D_MODEL), lambda i: (i, 0)), pl.BlockSpec((1, D_MODEL), lambda i: (0, 0))],
        out_specs=pl.BlockSpec((tm, D_MODEL), lambda i: (i, 0)),
        out_shape=jax.ShapeDtypeStruct((n, D_MODEL), F32),
        compiler_params=_cparams(("parallel",)),
        name="final_norm",
    )(x, g)


_NEW_PAD = LANES


def _rows_from_tokens(tok, n_rows, per):
    r = lax.broadcasted_iota(I32, (n_rows, LANES), 0)
    out = jnp.zeros((n_rows, LANES), F32)
    for i in range(n_rows // per):
        out = jnp.where(r // per == i, jnp.broadcast_to(tok[i:i + 1, :], (n_rows, LANES)), out)
    return out


def _rep_heads(x, row0, n_rows):
    r = lax.broadcasted_iota(I32, (n_rows, LANES), 0) & (FOX_HEADS - 1)
    out = jnp.zeros((n_rows, LANES), F32)
    for h in range(FOX_HEADS):
        out = jnp.where(r == h, jnp.broadcast_to(x[row0 + h:row0 + h + 1, :], (n_rows, LANES)), out)
    return out


def _softmax_update_wide(m_ref, l_ref, acc_ref, s, vt):
    m_old = m_ref[...]
    m_new = jnp.maximum(m_old, jnp.max(s, axis=1, keepdims=True))
    a = jnp.exp(m_old - m_new)
    p = jnp.exp(s - m_new[:, 0:1])
    l_ref[...] = a * l_ref[...] + jnp.sum(p, axis=1, keepdims=True)
    acc_ref[...] = a * acc_ref[...] + _dot_t(p.astype(BF16), vt)
    m_ref[...] = m_new


def _pick_lane(x, lane_of_row):
    lane = lax.broadcasted_iota(I32, x.shape, 1)
    return jnp.sum(jnp.where(lane == lane_of_row, x, 0.0), axis=1, keepdims=True)


def _decode1_kernel(pt_ref, qm_ref, fq_ref, newm_ref, newf_ref, cnew_ref, wuv_ref, *rest, pp, n_new):
    del pt_ref
    pages = rest[:4 * pp]
    omla_ref, ofox_ref, cmpm_ref = rest[4 * pp:4 * pp + 3]
    m1, l1, a1, m2, l2, a2, suf, newc = rest[4 * pp + 3:]
    jj = pl.program_id(1)
    r_m = MLA_HEADS * n_new
    r_f = FOX_HEADS * n_new
    row_f = lax.broadcasted_iota(I32, (r_f, LANES), 0)
    lane_f = lax.broadcasted_iota(I32, (r_f, LANES), 1)
    head_lane = _L_LOGF + (row_f & (FOX_HEADS - 1))

    @pl.when(jj == 0)
    def _():
        _softmax_init(m1, l1, a1)
        _softmax_init(m2, l2, a2)
        suf[...] = jnp.zeros(suf.shape, F32)
        x = cnew_ref[0]
        sub = lax.broadcasted_iota(I32, (SUBLANES, LANES), 0)
        y = x + jnp.where(sub >= 1, pltpu.roll(x, 1, 0), 0.0)
        y = y + jnp.where(sub >= 2, pltpu.roll(y, 2, 0), 0.0)
        newc[0:SUBLANES, :] = y
        col = _pick_lane(_rows_from_tokens(y, r_f, FOX_HEADS), head_lane)
        newc[SUBLANES:SUBLANES + r_f, :] = jnp.broadcast_to(col, (r_f, LANES))

    q = qm_ref[0]
    fq = fq_ref[0]
    ri = lax.broadcasted_iota(I32, (LANES, LANES), 0)
    ci_ = lax.broadcasted_iota(I32, (LANES, LANES), 1)
    later = jnp.where(ri > ci_, 1.0, 0.0).astype(BF16)
    blocks_per_page = PAGE_SIZE // CMP_BLOCK
    ra = lax.broadcasted_iota(I32, (2 * SUBLANES, LANES), 0)
    la = lax.broadcasted_iota(I32, (2 * SUBLANES, LANES), 1)
    in_block = jnp.where((la // CMP_BLOCK == ra) & (ra < blocks_per_page), 1.0, 0.0).astype(BF16)
    c_t, kr_t, kf_t, vf_t, lfs = [], [], [], [], []
    for i in range(pp):
        pm, pf, plf, pn = pages[4 * i:4 * i + 4]
        km = pm[0, 0]
        c_t.append(km[0:KV_RANK].astype(BF16))
        kr_t.append(km[KV_RANK:KV_RANK + ROPE_DIM].astype(BF16))
        kf = pf[0, 0]
        kf_t.append(kf[0:LANES].astype(BF16))
        vf_t.append(kf[LANES:2 * LANES].astype(BF16))
        lfs.append(plf[0, 0])
        kc = pn[0, 0]
        k1 = kc.astype(BF16)
        k2 = (kc - k1.astype(F32)).astype(BF16)
        means = (_dot_t(in_block, k1) + _dot_t(in_block, k2))[0:blocks_per_page] * (1.0 / CMP_BLOCK)
        cmpm_ref[0, pp - 1 - i] = means
    c_all = jnp.concatenate(c_t, axis=1)
    s = _dot(q[:, 0:KV_RANK], c_all) + _dot(q[:, KV_RANK:KV_RANK + ROPE_DIM], jnp.concatenate(kr_t, axis=1))
    _softmax_update_wide(m1, l1, a1, s, c_all)
    lf_all = jnp.concatenate(lfs, axis=0)
    n_lf = lf_all.shape[0]
    if n_lf % (2 * SUBLANES):
        lf_all = jnp.concatenate([lf_all, jnp.zeros((2 * SUBLANES - n_lf % (2 * SUBLANES), LANES), F32)], axis=0)
    x1, x2, x3 = _split3(lf_all)
    exc_all = (_dot(x1, later) + _dot(x2, later)) + _dot(x3, later)
    tot_all = jnp.broadcast_to(exc_all[:, 0:1] + lf_all[:, 0:1], exc_all.shape)
    after = suf[...]
    new_part = newc[SUBLANES:SUBLANES + r_f, :]
    decays = []
    for i in range(pp):
        decays.append(_rep_heads(exc_all, i * FOX_HEADS, r_f) + (after + new_part))
        after = after + _rep_heads(tot_all, i * FOX_HEADS, r_f)
    suf[...] = after
    s = _dot(fq, jnp.concatenate(kf_t, axis=1)) + jnp.concatenate(decays, axis=1)
    _softmax_update_wide(m2, l2, a2, s, jnp.concatenate(vf_t, axis=1))

    @pl.when(jj == pl.num_programs(1) - 1)
    def _():
        lane_m = lax.broadcasted_iota(I32, (r_m, _NEW_PAD), 1)
        row_m = lax.broadcasted_iota(I32, (r_m, _NEW_PAD), 0)
        kn = newm_ref[0]
        s = jnp.where((lane_m <= row_m // MLA_HEADS) & (lane_m < n_new), _dot_t(q, kn), NEG_INF)
        _softmax_update(m1, l1, a1, s, kn[:, 0:KV_RANK])
        o_lat = (a1[...] / l1[...]).astype(BF16)
        rr = lax.broadcasted_iota(I32, (r_m, LANES), 0)
        o = jnp.zeros((r_m, MLA_HEADS * MLA_V_DIM), F32)
        for h in range(MLA_HEADS):
            o = o + _dot(jnp.where((rr & (MLA_HEADS - 1)) == h, o_lat, jnp.zeros_like(o_lat)), wuv_ref[h])
        omla_ref[0] = jnp.sum(o.reshape(n_new, MLA_HEADS, MLA_HEADS * MLA_V_DIM), axis=1)

        kf = newf_ref[0]
        yc = newc[0:SUBLANES, :]
        ci = newc[SUBLANES:SUBLANES + r_f, :]
        lane_n = lax.broadcasted_iota(I32, (r_f, _NEW_PAD), 1)
        row_n = lax.broadcasted_iota(I32, (r_f, _NEW_PAD), 0)
        decay = jnp.zeros((r_f, _NEW_PAD), F32)
        for j in range(n_new):
            cj = _pick_lane(jnp.broadcast_to(yc[j:j + 1, :], (r_f, LANES)), head_lane)
            decay = jnp.where(lane_n == j, ci[:, 0:_NEW_PAD] - cj, decay)
        s = _dot_t(fq, kf[:, 0:LANES]) + decay
        s = jnp.where((lane_n <= row_n // FOX_HEADS) & (lane_n < n_new), s, NEG_INF)
        _softmax_update(m2, l2, a2, s, kf[:, LANES:2 * LANES])
        ofox_ref[0] = a2[...] / l2[...]


def _decode1(layer, page_table, qm, fq, newm, newf, cnew, wuv, cache_mla, cache_fox, logf_t, cache_nsa, pp):
    b, n_pages = page_table.shape
    n_new = qm.shape[1] // MLA_HEADS
    steps = n_pages // pp
    per_b = lambda a: pl.BlockSpec((1,) + a.shape[1:], lambda bi, jj, pt: (bi,) + (0,) * (a.ndim - 1))

    def page_spec(feats, i):
        def imap(bi, jj, pt):
            return (layer, pt[bi, n_pages - 1 - (jj * pp + i)], 0, 0)
        return pl.BlockSpec((1, 1, feats, PAGE_SIZE), imap)

    in_specs = [per_b(qm), per_b(fq), per_b(newm), per_b(newf), per_b(cnew),
                pl.BlockSpec(wuv.shape, lambda bi, jj, pt: (0, 0, 0))]
    args = [qm, fq, newm, newf, cnew, wuv]
    for i in range(pp):
        in_specs += [page_spec(KV_RANK + ROPE_DIM, i), page_spec(2 * LANES, i),
                     page_spec(FOX_HEADS, i), page_spec(LANES, i)]
        args += [cache_mla, cache_fox, logf_t, cache_nsa]
    r_m, r_f = MLA_HEADS * n_new, FOX_HEADS * n_new
    grid_spec = pltpu.PrefetchScalarGridSpec(
        num_scalar_prefetch=1, grid=(b, steps), in_specs=in_specs,
        out_specs=[pl.BlockSpec((1, n_new, MLA_HEADS * MLA_V_DIM), lambda bi, jj, pt: (bi, 0, 0)),
                   pl.BlockSpec((1, r_f, LANES), lambda bi, jj, pt: (bi, 0, 0)),
                   pl.BlockSpec((1, pp, PAGE_SIZE // CMP_BLOCK, LANES), lambda bi, jj, pt: (bi, steps - 1 - jj, 0, 0))],
        scratch_shapes=[pltpu.VMEM((r_m, LANES), F32), pltpu.VMEM((r_m, LANES), F32), pltpu.VMEM((r_m, LANES), F32),
                        pltpu.VMEM((r_f, LANES), F32), pltpu.VMEM((r_f, LANES), F32), pltpu.VMEM((r_f, LANES), F32),
                        pltpu.VMEM((r_f, LANES), F32), pltpu.VMEM((SUBLANES + r_f, LANES), F32)])
    return pl.pallas_call(
        functools.partial(_decode1_kernel, pp=pp, n_new=n_new),
        grid_spec=grid_spec,
        out_shape=[jax.ShapeDtypeStruct((b, n_new, MLA_HEADS * MLA_V_DIM), F32),
                   jax.ShapeDtypeStruct((b, r_f, LANES), F32),
                   jax.ShapeDtypeStruct((b, n_pages, PAGE_SIZE // CMP_BLOCK, LANES), F32)],
        compiler_params=_cparams(("parallel", "arbitrary"), 40),
        name="decode_mla_fox",
    )(page_table, *args)


def _decode2_kernel(pt_ref, nq_ref, misc_ref, cmp_ref, bkey_ref, bnear_ref, bcmp_ref, win_ref, neww_ref, newn_ref,
                    eblk_ref, *rest, pp, n_new):
    del pt_ref
    pages = rest[:pp]
    onsa_ref = rest[pp]
    selk, ocmp, m, l, acc = rest[pp + 1:]
    jj = pl.program_id(1)
    r_n = NSA_HEADS * n_new
    nq = nq_ref[0]
    row = lax.broadcasted_iota(I32, (r_n, LANES), 0)
    lane = lax.broadcasted_iota(I32, (r_n, LANES), 1)

    @pl.when(jj == 0)
    def _():
        cmpk = cmp_ref[0].astype(BF16)
        s = _dot_t(nq, cmpk) + bcmp_ref[...]
        p = jnp.exp(s - jnp.max(s, axis=1, keepdims=True))
        pc = p / jnp.sum(p, axis=1, keepdims=True)
        ocmp[...] = _dot(pc.astype(BF16), cmpk)
        gr = lax.broadcasted_iota(I32, (r_n, r_n), 0) // NSA_HEADS
        gc = lax.broadcasted_iota(I32, (r_n, r_n), 1) // NSA_HEADS
        same_q = jnp.where(gr == gc, 1.0, 0.0).astype(BF16)
        p1, p2, p3 = _split3(pc)
        pcs = (_dot(same_q, p1) + _dot(same_q, p2)) + _dot(same_q, p3)
        half = pcs.shape[1] // 2
        imp = pcs[:, 0:half] + pcs[:, half:2 * half]
        lane_b = lax.broadcasted_iota(I32, imp.shape, 1)
        imp = jnp.where(lane_b == 0, FORCE_SCORE, imp)
        n_sel = half + 1
        sel = _topk_mask_lanes(imp, lane_b.astype(F32), min(SEL_TOPK, n_sel) - 1)
        if half < LANES:
            sel = jnp.concatenate([sel, jnp.zeros((r_n, LANES - half), F32)], axis=1)
        key_sel = _dot(sel.astype(BF16), eblk_ref[...])
        step_w = pp * LANES
        for st in range(selk.shape[0]):
            selk[st] = key_sel[:, st * step_w:(st + 1) * step_w]
        _softmax_init(m, l, acc)

    k_all = jnp.concatenate([pages[i][0, 0].astype(BF16) for i in range(pp)], axis=1)
    s = _dot(nq, k_all) + bkey_ref[...]
    s = jnp.where(selk[jj] > 0.5, s, NEG_INF)
    _softmax_update_wide(m, l, acc, s, k_all)

    @pl.when(jj == pl.num_programs(1) - 1)
    def _():
        lane_n = lax.broadcasted_iota(I32, (r_n, _NEW_PAD), 1)
        row_n = lax.broadcasted_iota(I32, (r_n, _NEW_PAD), 0)
        own = (lane_n <= row_n // NSA_HEADS) & (lane_n < n_new)
        bnew = bnear_ref[:, WINDOW:WINDOW + LANES][:, 0:_NEW_PAD]
        kn = newn_ref[0]
        s = jnp.where(own, _dot_t(nq, kn) + bnew, NEG_INF)
        _softmax_update(m, l, acc, s, kn)
        o_sel = acc[...] / l[...]
        _softmax_init(m, l, acc)
        for c in range(WINDOW // LANES):
            kw = win_ref[0, c * LANES:(c + 1) * LANES, :].astype(BF16)
            s = _dot_t(nq, kw) + bnear_ref[:, c * LANES:(c + 1) * LANES]
            if c == 0:
                s = jnp.where(lane >= row // NSA_HEADS, s, NEG_INF)
            _softmax_update(m, l, acc, s, kw)
        kwn = neww_ref[0]
        s = jnp.where(own, _dot_t(nq, kwn) + bnew, NEG_INF)
        _softmax_update(m, l, acc, s, kwn)
        o_win = acc[...] / l[...]
        g16 = _rows_from_tokens(misc_ref[0], r_n, NSA_HEADS)
        gate = lambda c: _pick_lane(g16, _L_GATE + c * NSA_HEADS + (row & (NSA_HEADS - 1)))
        onsa_ref[0] = gate(0) * ocmp[...] + gate(1) * o_sel + gate(2) * o_win


def _decode2(layer, page_table, nq, misc, cmp_eo, bkey, bnear, bcmp, win, neww, newn, cache_nsa, pp):
    b, n_pages = page_table.shape
    r_n = nq.shape[1]
    n_new = r_n // NSA_HEADS
    steps = n_pages // pp
    per_b = lambda a: pl.BlockSpec((1,) + a.shape[1:], lambda bi, jj, pt: (bi,) + (0,) * (a.ndim - 1))
    const = lambda a: pl.BlockSpec(a.shape, lambda bi, jj, pt: (0,) * a.ndim)
    assert n_pages * PAGE_SIZE // SEL_BLOCK <= LANES
    eblk = (jnp.arange(n_pages * PAGE_SIZE, dtype=I32)[None, :] // SEL_BLOCK
            == jnp.arange(LANES, dtype=I32)[:, None]).astype(BF16)
    in_specs = [per_b(nq), per_b(misc), per_b(cmp_eo),
                pl.BlockSpec((r_n, pp * LANES), lambda bi, jj, pt: (0, jj)), const(bnear), const(bcmp),
                per_b(win), per_b(neww), per_b(newn), const(eblk)]
    args = [nq, misc, cmp_eo, bkey, bnear, bcmp, win, neww, newn, eblk]
    for i in range(pp):
        in_specs.append(pl.BlockSpec((1, 1, LANES, PAGE_SIZE),
                                     lambda bi, jj, pt, i=i: (layer, pt[bi, jj * pp + i], 1, 0)))
        args.append(cache_nsa)
    grid_spec = pltpu.PrefetchScalarGridSpec(
        num_scalar_prefetch=1, grid=(b, steps), in_specs=in_specs,
        out_specs=pl.BlockSpec((1, r_n, LANES), lambda bi, jj, pt: (bi, 0, 0)),
        scratch_shapes=[pltpu.VMEM((steps, r_n, pp * LANES), F32), pltpu.VMEM((r_n, LANES), F32),
                        pltpu.VMEM((r_n, LANES), F32), pltpu.VMEM((r_n, LANES), F32), pltpu.VMEM((r_n, LANES), F32)])
    return pl.pallas_call(
        functools.partial(_decode2_kernel, pp=pp, n_new=n_new),
        grid_spec=grid_spec,
        out_shape=jax.ShapeDtypeStruct((b, r_n, LANES), F32),
        compiler_params=_cparams(("parallel", "arbitrary"), 40),
        name="decode_nsa",
    )(page_table, *args)


def _rope_swap(w):
    half = ROPE_DIM // 2
    return jnp.concatenate([w[..., half:], w[..., :half]], axis=-1)


def _pad_lanes(w, width=LANES, at=0):
    out = jnp.zeros(w.shape[:-1] + (width,), w.dtype)
    return out.at[..., at:at + w.shape[-1]].set(w)


def _prep_layer(l, w_in, b_f, g_q, g_kv, g_mix, w_uq, w_uk, w_uv, w_o):
    offs = np.concatenate([[0], np.cumsum(IN_SPLITS)])
    col = lambda i: w_in[l][:, offs[i]:offs[i + 1]]
    cq, ckv, kr, fq, fk, fv, ff, nq, kc, vc, ks, vs, kw, vw, ng = [col(i) for i in range(len(IN_SPLITS))]
    ga = _pad_lanes(jnp.concatenate([kr, ff, ng], axis=1))
    gb = _pad_lanes(_rope_swap(kr))
    fq4 = [_pad_lanes(fq[:, h * HEAD_DIM:(h + 1) * HEAD_DIM], at=(h // FOX_GROUP) * HEAD_DIM) for h in range(FOX_HEADS)]
    nq4 = [_pad_lanes(nq[:, h * HEAD_DIM:(h + 1) * HEAD_DIM]) for h in range(NSA_HEADS)]
    w1 = jnp.concatenate([cq, ckv, ga, gb] + fq4 + [fk, fv] + nq4 + [kc, vc, ks, vs, kw, vw], axis=1).astype(BF16)
    uq = w_uq[l]
    nope = uq[:, :, :NOPE_DIM].reshape(Q_RANK, MLA_HEADS * NOPE_DIM)
    rope = _pad_lanes(uq[:, :, NOPE_DIM:]).reshape(Q_RANK, MLA_HEADS * LANES)
    rsw = _pad_lanes(_rope_swap(uq[:, :, NOPE_DIM:])).reshape(Q_RANK, MLA_HEADS * LANES)
    wq = jnp.concatenate([nope, rope, rsw], axis=1).astype(BF16)
    wuk = jnp.zeros((MLA_HEADS * NOPE_DIM, MLA_HEADS * KV_RANK), F32)
    wuv = jnp.zeros((MLA_HEADS, KV_RANK, MLA_HEADS * MLA_V_DIM), F32)
    for h in range(MLA_HEADS):
        wuk = wuk.at[h * NOPE_DIM:(h + 1) * NOPE_DIM, h * KV_RANK:(h + 1) * KV_RANK].set(w_uk[l][:, h, :].T)
        wuv = wuv.at[h, :, h * MLA_V_DIM:(h + 1) * MLA_V_DIM].set(w_uv[l][:, h, :])
    wo = jnp.zeros((_MIX_PAD, D_MODEL), F32)
    n_mla = MLA_HEADS * MLA_V_DIM
    wo = wo.at[0:n_mla].set(w_o[l][0:n_mla])
    for h in range(FOX_HEADS):
        r0 = n_mla + h * LANES + (h // FOX_GROUP) * HEAD_DIM
        wo = wo.at[r0:r0 + HEAD_DIM].set(w_o[l][n_mla + h * HEAD_DIM:n_mla + (h + 1) * HEAD_DIM])
    for h in range(NSA_HEADS):
        r0 = n_mla + (FOX_HEADS + h) * LANES + HEAD_DIM
        src = n_mla + (FOX_HEADS + h) * HEAD_DIM
        wo = wo.at[r0:r0 + HEAD_DIM].set(w_o[l][src:src + HEAD_DIM])
    return dict(w1=w1, wq=wq, wuk=wuk.astype(BF16), wuv=wuv.astype(BF16), wo=wo.astype(BF16),
                g_mix=g_mix[l][None], g_q=g_q[l][None], g_kv=g_kv[l][None],
                bf=_pad_lanes(b_f[l][None], at=_L_LOGF))


def _rope_tables(pos):
    half = ROPE_DIM // 2
    inv = ROPE_BASE ** (-jnp.arange(half, dtype=F32) / half)
    ang = pos.astype(F32)[:, None] * inv
    cos, sin = jnp.cos(ang), jnp.sin(ang)
    return _pad_lanes(jnp.concatenate([cos, cos], axis=1)), _pad_lanes(jnp.concatenate([-sin, sin], axis=1))


_PROJ_NAMES = ("mla", "fkv", "nkv", "win", "misc", "qm", "kvm", "fq", "kvf", "nq", "kvn", "kvw", "cmp")


def _pad_rows(a, rows):
    return jnp.pad(a, ((0, 0), (0, rows - a.shape[1]), (0, 0)))


def kernel(x_prompt, x_sample, c_prompt, c_sample, cache_mla, cache_fox_kv, cache_fox_logf, cache_nsa_kv, state_nsa_win, page_table, w_ada, b_ada, g_mix, g_ffn, g_final, w_in, b_f, g_q, g_kv, w_uq, w_uk, w_uv, w_o, t5_table, w_pq, sub_keys, expert_u, expert_v):
    bp, sp, d = x_prompt.shape
    bs, ss, _ = x_sample.shape
    depth = w_in.shape[0]
    n_pool = cache_mla.shape[1]
    n_pages = page_table.shape[1]
    past_len = n_pages * PAGE_SIZE
    assert state_nsa_win.shape[2] == WINDOW and sp % QBLK == 0 and sp > QBLK
    np_, ns_ = bp * sp, bs * ss
    tm_p = 256
    tm_s = min(256, ns_)
    pp = next(p for p in (8, 4, 2, 1) if n_pages % p == 0)

    xp = x_prompt.reshape(np_, d)
    xs = x_sample.reshape(ns_, d)
    cache_mla = cache_mla.swapaxes(2, 3)
    cache_fox = cache_fox_kv.reshape(depth, n_pool, PAGE_SIZE, 2 * FOX_KV_HEADS * HEAD_DIM).swapaxes(2, 3)
    cache_nsa = cache_nsa_kv.reshape(depth, n_pool, PAGE_SIZE, 4 * HEAD_DIM).swapaxes(2, 3)
    logf_t = cache_fox_logf.swapaxes(2, 3)
    win_buf = state_nsa_win.reshape(depth, bs, WINDOW, 2 * HEAD_DIM)

    cm_p, sm_p = _rope_tables(jnp.arange(sp, dtype=I32))
    cm_s, sm_s = _rope_tables(past_len + jnp.arange(ss, dtype=I32))
    cm_s, sm_s = jnp.tile(cm_s, (tm_s // ss, 1)), jnp.tile(sm_s, (tm_s // ss, 1))
    toep, cmpb = _t5_prompt(t5_table, sp)
    bkey, bnear, bcmp = _t5_decode(t5_table, past_len, ss, WINDOW)
    rows_ih = lambda t: t[:, :ss].transpose(1, 0, 2).reshape(ss * NSA_HEADS, t.shape[2])
    bkey, bnear, bcmp = rows_ih(bkey), rows_ih(bnear), rows_ih(bcmp)

    c_all = jnp.concatenate([c_prompt, c_sample], axis=0)
    st_p, st_s = [], []
    for l in range(depth):
        lw = _prep_layer(l, w_in, b_f, g_q, g_kv, g_mix, w_uq, w_uk, w_uv, w_o)
        mod = _ada(c_all, w_ada[l], b_ada[l][None])
        mod_p = [m[:, None, :] for m in jnp.split(mod[:bp], 6, axis=-1)]
        mod_s = [jnp.repeat(m, ss, axis=0).reshape(ns_ // tm_s, tm_s, d) for m in jnp.split(mod[bp:], 6, axis=-1)]
        wpq = w_pq[l].astype(BF16)
        keys = sub_keys[l].reshape(PEER_HEADS * 2, N_KEYS, PEER_KEY_DIM).astype(BF16)
        tbl_u, tbl_v = _pack_table(expert_u[l]), _pack_table(expert_v[l])

        po = dict(zip(_PROJ_NAMES, _proj(xp, mod_p[1], mod_p[0], lw, cm_p, sm_p, tm_p)))
        cumc, cumr = _cum(po["misc"].reshape(bp, sp, LANES))
        mix = _attn_prompt(po, cumc, cumr, toep, cmpb, lw["wuv"], bp, sp)
        xp = _wo(mix, xp, mod_p[2], lw["wo"], tm_p)
        xp = _peer_block(xp, mod_p[4], mod_p[3], mod_p[5], g_ffn[l][None], wpq, keys, tbl_u, tbl_v, tm_p, 128)
        w_keep = min(WINDOW, sp)
        st_p.append((po["mla"].reshape(bp, sp, -1),
                     po["fkv"].reshape(bp, sp, 2, FOX_KV_HEADS, HEAD_DIM),
                     po["misc"][:, _L_LOGF:_L_GATE].reshape(bp, sp, FOX_HEADS),
                     po["nkv"].reshape(bp, sp, 4, HEAD_DIM),
                     po["win"].reshape(bp, sp, 2, HEAD_DIM)[:, sp - w_keep:]))

        so = dict(zip(_PROJ_NAMES, _proj(xs, mod_s[1], mod_s[0], lw, cm_s, sm_s, tm_s)))
        per_tok = lambda a: a.reshape(bs, ss, a.shape[1])
        newk = lambda a: _pad_rows(per_tok(a), _NEW_PAD)
        qm = so["qm"].reshape(bs, ss * MLA_HEADS, 2 * LANES)
        fq = so["fq"].reshape(bs, ss * FOX_HEADS, LANES)
        nq = so["nq"].reshape(bs, ss * NSA_HEADS, LANES)
        misc8 = _pad_rows(per_tok(so["misc"]), SUBLANES)
        omla, ofox, cmpm = _decode1(l, page_table, qm, fq, newk(so["kvm"]), newk(so["kvf"]), misc8, lw["wuv"],
                                    cache_mla, cache_fox, logf_t, cache_nsa, pp)
        cmpm = cmpm.reshape(bs, n_pages * (PAGE_SIZE // CMP_BLOCK), LANES)
        cmp_eo = jnp.concatenate([cmpm[:, 0::2], cmpm[:, 1::2]], axis=1)
        onsa = _decode2(l, page_table, nq, misc8, cmp_eo, bkey, bnear, bcmp, win_buf[l],
                        newk(so["kvw"]), newk(so["kvn"][:, 2 * HEAD_DIM:]), cache_nsa, pp)
        mix_s = jnp.concatenate([omla.reshape(ns_, -1), ofox.reshape(ns_, -1), onsa.reshape(ns_, -1)],
                                axis=1).astype(BF16)
        xs = _wo(mix_s, xs, mod_s[2], lw["wo"], tm_s)
        xs = _peer_block(xs, mod_s[4], mod_s[3], mod_s[5], g_ffn[l][None], wpq, keys, tbl_u, tbl_v, tm_s, 128)
        win_new = so["win"].reshape(bs, ss, 2, HEAD_DIM)
        win_all = jnp.concatenate([state_nsa_win[l], win_new], axis=1)
        st_s.append((per_tok(so["mla"]),
                     so["fkv"].reshape(bs, ss, 2, FOX_KV_HEADS, HEAD_DIM),
                     so["misc"][:, _L_LOGF:_L_GATE].reshape(bs, ss, FOX_HEADS),
                     so["nkv"].reshape(bs, ss, 4, HEAD_DIM),
                     win_all[:, win_all.shape[1] - min(WINDOW, win_all.shape[1]):]))

    yp = _final_norm(xp, g_final[None], tm_p).reshape(bp, sp, d)
    ys = _final_norm(xs, g_final[None], tm_s).reshape(bs, ss, d)
    stack = lambda st, i: jnp.stack([s[i] for s in st])
    return (yp, ys,
            stack(st_p, 0), stack(st_s, 0), stack(st_p, 1), stack(st_s, 1), stack(st_p, 2), stack(st_s, 2),
            stack(st_p, 3), stack(st_s, 3), stack(st_p, 4), stack(st_s, 4))
```
